```python
import math
import jax, jax.numpy as jnp
from jax import lax
import numpy as np

D_MODEL = 1024
BATCH = 4
SEQ = 4096
DEPTH = 2
DEC_BATCH = 32
DEC_SEQ = 8
PAST_LEN = 16384
PAGE_SIZE = 128

N_META = 16
MIX_WIDTH = D_MODEL
SB_HEADS = 8
SB_WIDTH = MIX_WIDTH // 2
SB_HEAD_DIM = SB_WIDTH // SB_HEADS
SB_BIAS_INIT = -6.0
Q_BLOCK = 128
POOL_WINDOWS = (2, 4, 8, 16)
POOL_GROUPS = len(POOL_WINDOWS)
POOL_WIDTH = MIX_WIDTH // 2
POOL_GROUP_DIM = POOL_WIDTH // POOL_GROUPS
POOL_CTX = max(POOL_WINDOWS) - 1
LRU_WIDTH = D_MODEL
LRU_BLOCKS = 8
LRU_BLOCK_DIM = LRU_WIDTH // LRU_BLOCKS
LRU_CONV = 4
LRU_C = 8.0
FFN_DIM = 2816
FFN_CONV = 3
LN_EPS = 1e-5
DN_ALPHA = (2.0 * DEPTH) ** 0.25
DN_BETA = (8.0 * DEPTH) ** -0.25
N_SB_LAYERS = (DEPTH + 1) // 2
N_LRU_LAYERS = DEPTH // 2

kernel_name = 'hybrid_stickbreak_pool_rglru_convffn_step'


def _layer_norm(x, g, b):
    xf = x.astype(jnp.float32)
    mu = jnp.mean(xf, axis=-1, keepdims=True)
    xc = xf - mu
    var = jnp.mean(xc * xc, axis=-1, keepdims=True)
    y = xc * lax.rsqrt(var + LN_EPS) * g.astype(jnp.float32) + b.astype(jnp.float32)
    return y.astype(x.dtype)


def _causal_dwconv(u, ctx, w, b):
    width = w.shape[0]
    ext = jnp.concatenate([ctx.astype(u.dtype), u], axis=1)
    y = lax.conv_general_dilated(ext, w[:, None, :].astype(u.dtype), (1,), 'VALID',
                                 dimension_numbers=('NWC', 'WIO', 'NWC'),
                                 feature_group_count=u.shape[-1])
    return y + b.astype(u.dtype), ext[:, ext.shape[1] - (width - 1):]


def _sb_weights(z, mask):
    log_keep = jnp.where(mask, jax.nn.log_sigmoid(-z), 0.0)
    after = lax.cumsum(log_keep, axis=z.ndim - 1, reverse=True) - log_keep
    return jnp.where(mask, jnp.exp(jax.nn.log_sigmoid(z) + after), 0.0)


def _sb_prompt(q, k, v, bias):
    bsz, length, heads, hd = q.shape
    scale = hd ** -0.5
    hb = bias.astype(jnp.float32)[:, None, None]
    pos = jnp.arange(length)
    zm = jnp.einsum('bqhd,bkhd->bhqk', q[:, :N_META], k[:, :N_META],
                    preferred_element_type=jnp.float32) * scale + hb
    am = _sb_weights(zm, pos[None, :N_META] < pos[:N_META, None])
    o_meta = jnp.einsum('bhqk,bkhd->bqhd', am.astype(v.dtype), v[:, :N_META])
    n_real = length - N_META
    nb = n_real // Q_BLOCK
    q_blocks = q[:, N_META:].reshape(bsz, nb, Q_BLOCK, heads, hd).transpose(1, 0, 2, 3, 4)
    qpos_blocks = (N_META + jnp.arange(n_real)).reshape(nb, Q_BLOCK)

    def block(args):
        qb, qp = args
        zb = jnp.einsum('bqhd,bkhd->bhqk', qb, k, preferred_element_type=jnp.float32) * scale + hb
        ab = _sb_weights(zb, pos[None, :] < qp[:, None])
        return jnp.einsum('bhqk,bkhd->bqhd', ab.astype(v.dtype), v)

    o_real = lax.map(block, (q_blocks, qpos_blocks))
    o_real = o_real.transpose(1, 0, 2, 3, 4).reshape(bsz, n_real, heads, hd)
    return jnp.concatenate([o_meta, o_real], axis=1)


def _sb_sample(q, k, v, cache_k, cache_v, page_table, bias):
    bsz, t, heads, hd = q.shape
    scale = hd ** -0.5
    hb = bias.astype(jnp.float32)[:, None, None]
    past = page_table.shape[1] * cache_k.shape[1]
    k_past = cache_k[page_table].reshape(bsz, past, heads, hd)
    v_past = cache_v[page_table].reshape(bsz, past, heads, hd)
    z = jnp.concatenate([
        jnp.einsum('bqhd,bkhd->bhqk', q, k_past.astype(q.dtype), preferred_element_type=jnp.float32),
        jnp.einsum('bqhd,bkhd->bhqk', q, k, preferred_element_type=jnp.float32)], axis=-1) * scale + hb
    qpos = past + jnp.arange(t)
    kpos = jnp.arange(past + t)
    a = _sb_weights(z, kpos[None, :] < qpos[:, None]).astype(v.dtype)
    return (jnp.einsum('bhqk,bkhd->bqhd', a[..., :past], v_past.astype(v.dtype))
            + jnp.einsum('bhqk,bkhd->bqhd', a[..., past:], v))


def _multiscale_pool(u, ctx, pos0, w_grp, scale):
    bsz, t, c = u.shape
    ext = jnp.concatenate([ctx.astype(u.dtype), u], axis=1)
    ext32 = ext.astype(jnp.float32)
    cs = jnp.concatenate([jnp.zeros((bsz, 1, c), jnp.float32), jnp.cumsum(ext32, axis=1)], axis=1)
    pos = pos0 + jnp.arange(t)
    means = []
    for g, w in enumerate(POOL_WINDOWS):
        sl = slice(g * POOL_GROUP_DIM, (g + 1) * POOL_GROUP_DIM)
        hi = cs[:, POOL_CTX + 1:POOL_CTX + 1 + t, sl]
        lo = cs[:, POOL_CTX + 1 - w:POOL_CTX + 1 - w + t, sl]
        cnt = jnp.minimum(pos + 1, w).astype(jnp.float32)[None, :, None]
        means.append((hi - lo) / cnt)
    d = (jnp.concatenate(means, axis=-1) - ext32[:, POOL_CTX:]).reshape(bsz, t, POOL_GROUPS, POOL_GROUP_DIM)
    mixed = jnp.einsum('btgc,gcd->btgd', d, w_grp.astype(jnp.float32)).reshape(bsz, t, c)
    return (mixed * scale.astype(jnp.float32)).astype(u.dtype), ext[:, ext.shape[1] - POOL_CTX:]


def _rg_lru(xc, h0, w_a, b_a, w_x, b_x, lam):
    bsz, t, r = xc.shape
    f32 = jnp.float32
    xf = xc.astype(f32)
    xb = xf.reshape(bsz, t, LRU_BLOCKS, LRU_BLOCK_DIM)
    gate_r = jax.nn.sigmoid(jnp.einsum('btnc,ncd->btnd', xb, w_a.astype(f32))
                            + b_a.astype(f32).reshape(LRU_BLOCKS, LRU_BLOCK_DIM)).reshape(bsz, t, r)
    gate_i = jax.nn.sigmoid(jnp.einsum('btnc,ncd->btnd', xb, w_x.astype(f32))
                            + b_x.astype(f32).reshape(LRU_BLOCKS, LRU_BLOCK_DIM)).reshape(bsz, t, r)
    log_a = -LRU_C * gate_r * jax.nn.softplus(-lam.astype(f32))
    a = jnp.exp(log_a)
    b = jnp.sqrt(-jnp.expm1(2.0 * log_a)) * gate_i * xf
    b = b.at[:, 0].add(a[:, 0] * h0.astype(f32))

    def combine(left, right):
        a1, b1 = left
        a2, b2 = right
        return a1 * a2, a2 * b1 + b2

    _, h = lax.associative_scan(combine, (a, b), axis=1)
    return h.astype(xc.dtype), h[:, -1].astype(xc.dtype)


def _conv_ffn(x, ctx, w_up, conv_w, conv_b, w_down):
    g, v = jnp.split(x @ w_up, 2, axis=-1)
    gc, new_ctx = _causal_dwconv(g, ctx, conv_w, conv_b)
    return (jax.nn.gelu(gc) * v) @ w_down, new_ctx


def _trunk(x, pos0, attend, pool_ctx, lru_conv_ctx, lru_h0, ffn_ctx, p):
    bsz, t, _ = x.shape
    new_k, new_v, new_pool, new_conv, new_h, new_ffn = [], [], [], [], [], []
    for layer in range(DEPTH):
        j = layer // 2
        if layer % 2 == 0:
            proj = x @ p['sb_w_in'][j]
            q, k, v, u = jnp.split(proj, [SB_WIDTH, 2 * SB_WIDTH, 3 * SB_WIDTH], axis=-1)
            q = q.reshape(bsz, t, SB_HEADS, SB_HEAD_DIM)
            k = k.reshape(bsz, t, SB_HEADS, SB_HEAD_DIM)
            v = v.reshape(bsz, t, SB_HEADS, SB_HEAD_DIM)
            o_a = attend(j, q, k, v, p['sb_logit_bias'][j]).reshape(bsz, t, SB_WIDTH)
            o_b, pc = _multiscale_pool(u, pool_ctx[j], pos0, p['pool_w'][j], p['pool_scale'][j])
            out = jnp.concatenate([o_a, o_b], axis=-1) @ p['sb_w_out'][j]
            new_k.append(k)
            new_v.append(v)
            new_pool.append(pc)
        else:
            gate, rec = jnp.split(x @ p['lru_w_in'][j], 2, axis=-1)
            xc, cc = _causal_dwconv(rec, lru_conv_ctx[j], p['lru_conv_w'][j], p['lru_conv_b'][j])
            h, hl = _rg_lru(xc, lru_h0[j], p['lru_w_a'][j], p['lru_b_a'][j], p['lru_w_x'][j],
                            p['lru_b_x'][j], p['lru_lambda'][j])
            out = (jax.nn.gelu(gate) * h) @ p['lru_w_out'][j]
            new_conv.append(cc)
            new_h.append(hl)
        x = _layer_norm(DN_ALPHA * x + out, p['ln_g'][layer, 0], p['ln_b'][layer, 0])
        f, fc = _conv_ffn(x, ffn_ctx[layer], p['ffn_w_up'][layer], p['ffn_conv_w'][layer],
                          p['ffn_conv_b'][layer], p['ffn_w_down'][layer])
        new_ffn.append(fc)
        x = _layer_norm(DN_ALPHA * x + f, p['ln_g'][layer, 1], p['ln_b'][layer, 1])
    return (x, jnp.stack(new_k), jnp.stack(new_v), jnp.stack(new_pool), jnp.stack(new_conv),
            jnp.stack(new_h), jnp.stack(new_ffn))


def setup_inputs(seed: int = 0) -> dict:
    key = jax.random.key(seed)
    ks = jax.random.split(key, 30)
    n_pages = PAST_LEN // PAGE_SIZE
    used = DEC_BATCH * n_pages
    n_phys = used + max(1, used // 4)
    nrm = jax.random.normal
    f32 = jnp.float32
    page_table = jax.random.permutation(ks[4], n_phys)[:used].reshape(DEC_BATCH, n_pages).astype(jnp.int32)
    u = jax.random.uniform(ks[21], (N_LRU_LAYERS, LRU_WIDTH), f32, 0.9, 0.999)
    s = u ** (1.0 / LRU_C)
    lru_lambda = jnp.log(s) - jnp.log1p(-s)
    return {
        'x_prompt': nrm(ks[0], (BATCH, SEQ, D_MODEL), f32),
        'x_sample': nrm(ks[1], (DEC_BATCH, DEC_SEQ, D_MODEL), f32),
        'cache_sb_k': nrm(ks[2], (N_SB_LAYERS, n_phys, PAGE_SIZE, SB_HEADS, SB_HEAD_DIM), f32),
        'cache_sb_v': nrm(ks[3], (N_SB_LAYERS, n_phys, PAGE_SIZE, SB_HEADS, SB_HEAD_DIM), f32),
        'page_table': page_table,
        'state_pool': nrm(ks[5], (N_SB_LAYERS, DEC_BATCH, POOL_CTX, POOL_WIDTH), f32),
        'state_lru_conv': nrm(ks[6], (N_LRU_LAYERS, DEC_BATCH, LRU_CONV - 1, LRU_WIDTH), f32),
        'state_lru_h': 0.5 * nrm(ks[7], (N_LRU_LAYERS, DEC_BATCH, LRU_WIDTH), f32),
        'state_ffn_conv': nrm(ks[8], (DEPTH, DEC_BATCH, FFN_CONV - 1, FFN_DIM), f32),
        'meta_tokens': nrm(ks[9], (N_META, D_MODEL), f32),
        'sb_w_in': nrm(ks[10], (N_SB_LAYERS, D_MODEL, 3 * SB_WIDTH + POOL_WIDTH), f32) * D_MODEL ** -0.5,
        'sb_logit_bias': SB_BIAS_INIT + 0.1 * nrm(ks[29], (N_SB_LAYERS, SB_HEADS), f32),
        'sb_w_out': nrm(ks[11], (N_SB_LAYERS, SB_WIDTH + POOL_WIDTH, D_MODEL), f32)
                    * ((SB_WIDTH + POOL_WIDTH) ** -0.5 * DN_BETA),
        'pool_w': nrm(ks[12], (N_SB_LAYERS, POOL_GROUPS, POOL_GROUP_DIM, POOL_GROUP_DIM), f32)
                  * POOL_GROUP_DIM ** -0.5,
        'pool_scale': 1.0 + 0.1 * nrm(ks[13], (N_SB_LAYERS, POOL_WIDTH), f32),
        'lru_w_in': nrm(ks[14], (N_LRU_LAYERS, D_MODEL, 2 * LRU_WIDTH), f32) * D_MODEL ** -0.5,
        'lru_conv_w': nrm(ks[15], (N_LRU_LAYERS, LRU_CONV, LRU_WIDTH), f32) * LRU_CONV ** -0.5,
        'lru_conv_b': 0.01 * nrm(ks[16], (N_LRU_LAYERS, LRU_WIDTH), f32),
        'lru_w_a': nrm(ks[17], (N_LRU_LAYERS, LRU_BLOCKS, LRU_BLOCK_DIM, LRU_BLOCK_DIM), f32)
                   * LRU_BLOCK_DIM ** -0.5,
        'lru_b_a': 0.01 * nrm(ks[18], (N_LRU_LAYERS, LRU_WIDTH), f32),
        'lru_w_x': nrm(ks[19], (N_LRU_LAYERS, LRU_BLOCKS, LRU_BLOCK_DIM, LRU_BLOCK_DIM), f32)
                   * LRU_BLOCK_DIM ** -0.5,
        'lru_b_x': 0.01 * nrm(ks[20], (N_LRU_LAYERS, LRU_WIDTH), f32),
        'lru_lambda': lru_lambda,
        'lru_w_out': nrm(ks[22], (N_LRU_LAYERS, LRU_WIDTH, D_MODEL), f32) * (LRU_WIDTH ** -0.5 * DN_BETA),
        'ffn_w_up': nrm(ks[23], (DEPTH, D_MODEL, 2 * FFN_DIM), f32) * D_MODEL ** -0.5,
        'ffn_conv_w': nrm(ks[24], (DEPTH, FFN_CONV, FFN_DIM), f32) * FFN_CONV ** -0.5,
        'ffn_conv_b': 0.01 * nrm(ks[25], (DEPTH, FFN_DIM), f32),
        'ffn_w_down': nrm(ks[26], (DEPTH, FFN_DIM, D_MODEL), f32) * (FFN_DIM ** -0.5 * DN_BETA),
        'ln_g': 1.0 + 0.05 * nrm(ks[27], (DEPTH, 2, D_MODEL), f32),
        'ln_b': 0.02 * nrm(ks[28], (DEPTH, 2, D_MODEL), f32),
    }


def reference(x_prompt, x_sample, cache_sb_k, cache_sb_v, page_table, state_pool, state_lru_conv,
              state_lru_h, state_ffn_conv, meta_tokens, sb_w_in, sb_logit_bias, sb_w_out, pool_w, pool_scale,
              lru_w_in, lru_conv_w, lru_conv_b, lru_w_a, lru_b_a, lru_w_x, lru_b_x, lru_lambda,
              lru_w_out, ffn_w_up, ffn_conv_w, ffn_conv_b, ffn_w_down, ln_g, ln_b):
    p = dict(sb_w_in=sb_w_in, sb_logit_bias=sb_logit_bias, sb_w_out=sb_w_out, pool_w=pool_w,
             pool_scale=pool_scale, lru_w_in=lru_w_in, lru_conv_w=lru_conv_w, lru_conv_b=lru_conv_b,
             lru_w_a=lru_w_a, lru_b_a=lru_b_a, lru_w_x=lru_w_x, lru_b_x=lru_b_x, lru_lambda=lru_lambda,
             lru_w_out=lru_w_out, ffn_w_up=ffn_w_up, ffn_conv_w=ffn_conv_w, ffn_conv_b=ffn_conv_b,
             ffn_w_down=ffn_w_down, ln_g=ln_g, ln_b=ln_b)
    dt = x_prompt.dtype
    bp = x_prompt.shape[0]
    xp = jnp.concatenate([jnp.broadcast_to(meta_tokens[None].astype(dt), (bp, N_META, D_MODEL)), x_prompt], axis=1)
    (hp, k_prompt, v_prompt, pool_prompt, lru_conv_prompt, lru_h_prompt, ffn_conv_prompt) = _trunk(
        xp, 0, lambda j, q, k, v, bias: _sb_prompt(q, k, v, bias),
        jnp.zeros((N_SB_LAYERS, bp, POOL_CTX, POOL_WIDTH), dt),
        jnp.zeros((N_LRU_LAYERS, bp, LRU_CONV - 1, LRU_WIDTH), dt),
        jnp.zeros((N_LRU_LAYERS, bp, LRU_WIDTH), dt),
        jnp.zeros((DEPTH, bp, FFN_CONV - 1, FFN_DIM), dt), p)
    y_prompt = hp[:, N_META:]
    past_len = page_table.shape[1] * cache_sb_k.shape[2]
    (y_sample, k_sample, v_sample, pool_sample, lru_conv_sample, lru_h_sample, ffn_conv_sample) = _trunk(
        x_sample, past_len,
        lambda j, q, k, v, bias: _sb_sample(q, k, v, cache_sb_k[j], cache_sb_v[j], page_table, bias),
        state_pool, state_lru_conv, state_lru_h, state_ffn_conv, p)
    return (y_prompt, y_sample, k_prompt, v_prompt, k_sample, v_sample, pool_prompt, pool_sample,
            lru_conv_prompt, lru_conv_sample, lru_h_prompt, lru_h_sample, ffn_conv_prompt, ffn_conv_sample)
```

```python
import functools

import jax
import jax.numpy as jnp
from jax import lax
from jax.experimental import pallas as pl
from jax.experimental.pallas import tpu as pltpu

F32 = jnp.float32
BF16 = jnp.bfloat16

D_MODEL = 1024
N_META = 16
SB_HEADS = 8
SB_HEAD_DIM = 64
SB_WIDTH = SB_HEADS * SB_HEAD_DIM
HEAD_PAIRS = SB_HEADS // 2
POOL_WINDOWS = (2, 4, 8, 16)
POOL_WIDTH = 512
POOL_GROUP_DIM = 128
POOL_CTX = 15
LRU_WIDTH = 1024
LRU_BLOCKS = 8
LRU_BLOCK_DIM = 128
LRU_CONV = 4
LRU_C = 8.0
FFN_DIM = 2816
FFN_CONV = 3
FFN_CHUNK = 256
FFN_CHUNKS = FFN_DIM // FFN_CHUNK
LN_EPS = 1e-5
DEPTH = 2
DN_ALPHA = (2.0 * DEPTH) ** 0.25
PAGE_SIZE = 128

LANES = 128
SUBLANES = 8
VMEM_LIMIT = 56 * 1024 * 1024


def _round_up(x, m):
    return (x + m - 1) // m * m


def _params(n_axes):
    return pltpu.CompilerParams(dimension_semantics=("arbitrary",) * n_axes,
                                vmem_limit_bytes=VMEM_LIMIT)


def _full_spec(shape):
    nd = len(shape)
    return pl.BlockSpec(shape, lambda *_: (0,) * nd)


def _layer_norm(y, g, b):
    mu = jnp.mean(y, axis=-1, keepdims=True)
    yc = y - mu
    var = jnp.mean(yc * yc, axis=-1, keepdims=True)
    return yc * lax.rsqrt(var + LN_EPS) * g + b


def _softplus(z):
    return jnp.maximum(z, 0.0) + jnp.log(1.0 + jnp.exp(-jnp.abs(z)))


def _dot(a, b):
    return jnp.dot(a, b, preferred_element_type=F32)


def _dot_nt(a, b):
    return lax.dot_general(a, b, (((1,), (1,)), ((), ())), preferred_element_type=F32)


def _neg_upper(n):
    j = lax.broadcasted_iota(jnp.int32, (n, n), 0)
    s = lax.broadcasted_iota(jnp.int32, (n, n), 1)
    return jnp.where(j >= s, -1.0, 0.0).astype(BF16)


def _sb_block(qh, kt, vt, un, bias, carry, mask):
    z = _dot_nt(qh, kt) + bias
    sp = _softplus(z)
    if mask is not None:
        sp = jnp.where(mask, sp, 0.0)
    incl = _dot(sp.astype(BF16), un)
    a = jnp.exp(z + incl + carry)
    if mask is not None:
        a = jnp.where(mask, a, 0.0)
    pv = _dot(a.astype(BF16), vt)
    return pv, carry - jnp.sum(sp, axis=-1, keepdims=True)


def _sb_in_kernel(x_ref, w_ref, q_ref, k_ref, v_ref, kb_ref, vb_ref, u_ref):
    xb = x_ref[...].astype(BF16)
    q = _dot(xb, w_ref[:, 0:SB_WIDTH])
    q_ref[...] = (q * (SB_HEAD_DIM ** -0.5)).astype(BF16)
    k = _dot(xb, w_ref[:, SB_WIDTH:2 * SB_WIDTH])
    k_ref[...] = k
    kb_ref[...] = k.astype(BF16)
    v = _dot(xb, w_ref[:, 2 * SB_WIDTH:3 * SB_WIDTH])
    v_ref[...] = v
    vb_ref[...] = v.astype(BF16)
    u_ref[...] = _dot(xb, w_ref[:, 3 * SB_WIDTH:3 * SB_WIDTH + POOL_WIDTH])


def _sb_in_call(x2d, w_in, tm):
    rows = x2d.shape[0]
    row_spec = lambda c: pl.BlockSpec((tm, c), lambda i: (i, 0))
    return pl.pallas_call(
        _sb_in_kernel,
        grid=(rows // tm,),
        in_specs=[row_spec(D_MODEL), _full_spec(w_in.shape)],
        out_specs=[row_spec(SB_WIDTH)] * 5 + [row_spec(POOL_WIDTH)],
        out_shape=[jax.ShapeDtypeStruct((rows, SB_WIDTH), BF16),
                   jax.ShapeDtypeStruct((rows, SB_WIDTH), F32),
                   jax.ShapeDtypeStruct((rows, SB_WIDTH), F32),
                   jax.ShapeDtypeStruct((rows, SB_WIDTH), BF16),
                   jax.ShapeDtypeStruct((rows, SB_WIDTH), BF16),
                   jax.ShapeDtypeStruct((rows, POOL_WIDTH), F32)],
        compiler_params=_params(1),
        name="sb_in",
    )(x2d, w_in)


def _sb_attn_kernel(*refs, tq, tk, n_past):
    if n_past:
        bias_ref, q_ref, k_ref, v_ref, un_ref, kp_ref, vp_ref, unp_ref, o_ref, acc_ref, car_ref = refs
    else:
        bias_ref, q_ref, k_ref, v_ref, un_ref, o_ref, acc_ref, car_ref = refs
    hp = pl.program_id(1)
    qi = pl.program_id(2)
    lane = lax.broadcasted_iota(jnp.int32, (tq, LANES), 1)
    q = q_ref[0]
    zero = jnp.zeros_like(q)
    qh = (jnp.where(lane < SB_HEAD_DIM, q, zero), jnp.where(lane >= SB_HEAD_DIM, q, zero))
    bias = (bias_ref[2 * hp], bias_ref[2 * hp + 1])
    un = un_ref[...]

    acc_ref[...] = jnp.zeros_like(acc_ref)
    car_ref[...] = jnp.zeros_like(car_ref)

    def visit(kt, vt, u, mask):
        for h in range(2):
            pv, car = _sb_block(qh[h], kt, vt, u, bias[h], car_ref[h], mask)
            acc_ref[h] += pv
            car_ref[h] = car

    n_diag = tq // tk
    row = lax.broadcasted_iota(jnp.int32, (tq, tk), 0)
    col = lax.broadcasted_iota(jnp.int32, (tq, tk), 1)
    for d in reversed(range(n_diag)):
        start = pl.multiple_of(qi * tq + d * tk, tk)
        visit(k_ref[0, pl.ds(start, tk), :], v_ref[0, pl.ds(start, tk), :], un, col + d * tk < row)

    def body(jj, carry):
        start = pl.multiple_of((qi * n_diag - 1 - jj) * tk, tk)
        visit(k_ref[0, pl.ds(start, tk), :], v_ref[0, pl.ds(start, tk), :], un, None)
        return carry

    lax.fori_loop(0, qi * n_diag, body, 0)

    if n_past:
        pcol = lax.broadcasted_iota(jnp.int32, (tq, kp_ref.shape[0]), 1)
        visit(kp_ref[...], vp_ref[...], unp_ref[...], pcol < n_past)

    o_ref[0] = jnp.where(lane < SB_HEAD_DIM, acc_ref[0], acc_ref[1]).astype(BF16)


def _sb_attn_call(bias, q, k, v, past, tq, tk):
    nb, rows, _ = q.shape
    tile_spec = pl.BlockSpec((1, tq, LANES), lambda b, h, i: (b, i, h))
    seq_spec = pl.BlockSpec((1, rows, LANES), lambda b, h, i: (b, 0, h))
    in_specs = [pl.BlockSpec(memory_space=pltpu.SMEM), tile_spec, seq_spec, seq_spec, _full_spec((tk, tk))]
    args = [bias, q, k, v, _neg_upper(tk)]
    n_past = 0
    if past is not None:
        kp, vp, n_past = past
        pp = kp.shape[0]
        past_spec = pl.BlockSpec((pp, LANES), lambda b, h, i: (0, h))
        in_specs += [past_spec, past_spec, _full_spec((pp, pp))]
        args += [kp, vp, _neg_upper(pp)]
    return pl.pallas_call(
        functools.partial(_sb_attn_kernel, tq=tq, tk=tk, n_past=n_past),
        grid=(nb, HEAD_PAIRS, rows // tq),
        in_specs=in_specs,
        out_specs=tile_spec,
        out_shape=jax.ShapeDtypeStruct((nb, rows, SB_WIDTH), BF16),
        scratch_shapes=[pltpu.VMEM((2, tq, LANES), F32), pltpu.VMEM((2, tq, 1), F32)],
        compiler_params=_params(3),
        name="sb_attn",
    )(*args)


PAGES_PER_STEP = 8
QROWS = SB_HEADS * 8


def _sb_sample_kernel(pt_ref, q_ref, kn_ref, vn_ref, bias_ref, un_ref, *refs, n_steps, t_new):
    del pt_ref
    page_refs = refs[:2 * PAGES_PER_STEP]
    o_ref, qs_ref, new_ref, acc_ref, car_ref = refs[2 * PAGES_PER_STEP:]
    j = pl.program_id(1)
    head_of_lane = lax.broadcasted_iota(jnp.int32, (t_new, SB_WIDTH), 1) // SB_HEAD_DIM
    bias = bias_ref[...]
    un = un_ref[...]

    def visit(kt, vt, mask):
        pv, car = _sb_block(qs_ref[...], kt, vt, un, bias, car_ref[...], mask)
        acc_ref[...] += pv
        car_ref[...] = car

    @pl.when(j == 0)
    def _():
        q = q_ref[0]
        for h in range(SB_HEADS):
            qs_ref[h * t_new:(h + 1) * t_new, :] = jnp.where(head_of_lane == h, q, 0.0)
        acc_ref[...] = jnp.zeros_like(acc_ref)
        car_ref[...] = jnp.zeros_like(car_ref)
        new_ref[...] = jnp.zeros_like(new_ref)
        new_ref[0, 0:t_new, :] = kn_ref[0]
        new_ref[1, 0:t_new, :] = vn_ref[0]
        t_of_row = lax.broadcasted_iota(jnp.int32, (QROWS, PAGE_SIZE), 0) % t_new
        col = lax.broadcasted_iota(jnp.int32, (QROWS, PAGE_SIZE), 1)
        visit(new_ref[0].astype(BF16), new_ref[1].astype(BF16), col < t_of_row)

    for p in range(PAGES_PER_STEP):
        visit(page_refs[p][0].astype(BF16), page_refs[PAGES_PER_STEP + p][0].astype(BF16), None)

    @pl.when(j == n_steps - 1)
    def _():
        out = jnp.zeros((t_new, SB_WIDTH), F32)
        for h in range(SB_HEADS):
            out = out + jnp.where(head_of_lane == h, acc_ref[h * t_new:(h + 1) * t_new, :], 0.0)
        o_ref[0] = out


def _sb_sample_call(page_table, q, kn, vn, bias, cache_k, cache_v):
    nb, t_new, _ = q.shape
    n_pages = page_table.shape[1]
    n_steps = n_pages // PAGES_PER_STEP
    tok_spec = pl.BlockSpec((1, t_new, SB_WIDTH), lambda b, j, pt: (b, 0, 0))

    def page_spec(p):
        return pl.BlockSpec((1, PAGE_SIZE, SB_WIDTH),
                            lambda b, j, pt: (pt[b, n_pages - 1 - (j * PAGES_PER_STEP + p)], 0, 0))

    page_specs = [page_spec(p) for p in range(PAGES_PER_STEP)]
    bias_rows = jnp.repeat(bias.astype(F32), t_new)[:, None]
    grid_spec = pltpu.PrefetchScalarGridSpec(
        num_scalar_prefetch=1,
        grid=(nb, n_steps),
        in_specs=[tok_spec, tok_spec, tok_spec,
                  pl.BlockSpec((QROWS, 1), lambda b, j, pt: (0, 0)),
                  pl.BlockSpec((PAGE_SIZE, PAGE_SIZE), lambda b, j, pt: (0, 0))] + page_specs + page_specs,
        out_specs=tok_spec,
        scratch_shapes=[pltpu.VMEM((QROWS, SB_WIDTH), F32), pltpu.VMEM((2, PAGE_SIZE, SB_WIDTH), F32),
                        pltpu.VMEM((QROWS, SB_WIDTH), F32), pltpu.VMEM((QROWS, 1), F32)],
    )
    return pl.pallas_call(
        functools.partial(_sb_sample_kernel, n_steps=n_steps, t_new=t_new),
        grid_spec=grid_spec,
        out_shape=jax.ShapeDtypeStruct((nb, t_new, SB_WIDTH), F32),
        compiler_params=_params(2),
        name="sb_sample",
    )(page_table, q, kn, vn, bias_rows, _neg_upper(PAGE_SIZE),
      *([cache_k] * PAGES_PER_STEP), *([cache_v] * PAGES_PER_STEP))


def _pool_layout(ts):
    if ts == 1:
        return 32, (8, 16, 24, 32)
    halo = POOL_CTX * ts
    return halo, (halo - 14 * ts, halo - 12 * ts, halo - 8 * ts, halo)


def _sb_out_kernel(oa_ref, u_ref, pctx_ref, x_ref, wo_ref, pw_ref, ps_ref, g_ref, b_ref,
                   y_ref, nctx_ref, ext_ref, pa_ref, pb_ref, *, tm, ts, n_tiles, end_off, pos0):
    i = pl.program_id(1)
    halo, starts = _pool_layout(ts)
    rows = halo + tm
    ctx_rows = POOL_CTX * ts

    @pl.when(i == 0)
    def _():
        if halo > ctx_rows:
            ext_ref[0:halo - ctx_rows, :] = jnp.zeros((halo - ctx_rows, POOL_WIDTH), F32)
        ext_ref[halo - ctx_rows:halo, :] = pctx_ref[0]

    u = u_ref[0]
    ext_ref[halo:rows, :] = u

    src = ext_ref
    bufs = (pa_ref, pb_ref, pa_ref, pb_ref)
    for s in range(4):
        lo = starts[s]
        shift = (1 << s) * ts
        c0 = s * POOL_GROUP_DIM
        dst = bufs[s]
        dst[lo:rows, c0:] = src[lo:rows, c0:] + src[lo - shift:rows - shift, c0:]
        src = dst

    if pos0 + 1 >= POOL_WINDOWS[-1]:
        inv = [1.0 / w for w in POOL_WINDOWS]
    else:
        pos = pos0 + i * (tm // ts) + lax.broadcasted_iota(jnp.int32, (tm, POOL_GROUP_DIM), 0) // ts
        inv = [1.0 / jnp.minimum(pos + 1, w).astype(F32) for w in POOL_WINDOWS]

    out = _dot(oa_ref[0], wo_ref[0:SB_WIDTH, :])
    for g in range(4):
        c = slice(g * POOL_GROUP_DIM, (g + 1) * POOL_GROUP_DIM)
        d = bufs[g][halo:rows, c] * inv[g] - u[:, c]
        mixed = _dot(d.astype(BF16), pw_ref[g]) * ps_ref[:, c]
        out = out + _dot(mixed.astype(BF16), wo_ref[SB_WIDTH + g * POOL_GROUP_DIM:SB_WIDTH + (g + 1) * POOL_GROUP_DIM, :])

    y_ref[0] = _layer_norm(DN_ALPHA * x_ref[0] + out, g_ref[...], b_ref[...])

    @pl.when(i == n_tiles - 1)
    def _():
        nctx_ref[0] = ext_ref[halo + end_off - ctx_rows:halo + end_off, :]

    if n_tiles > 1:
        ext_ref[halo - 16:halo, :] = ext_ref[rows - 16:rows, :]


def _sb_out_call(oa, u, pctx, x, wo, pw, ps, g, b, *, tm, ts, end, pos0):
    nb, rows, _ = x.shape
    n_tiles = rows // tm
    assert n_tiles == 1 or ts == 1
    assert rows - tm < end <= rows
    halo, _ = _pool_layout(ts)
    ctx_rows = POOL_CTX * ts
    tile = lambda c: pl.BlockSpec((1, tm, c), lambda bi, i: (bi, i, 0))
    ctx_spec = pl.BlockSpec((1, ctx_rows, POOL_WIDTH), lambda bi, i: (bi, 0, 0))
    return pl.pallas_call(
        functools.partial(_sb_out_kernel, tm=tm, ts=ts, n_tiles=n_tiles, end_off=end - (rows - tm), pos0=pos0),
        grid=(nb, n_tiles),
        in_specs=[tile(SB_WIDTH), tile(POOL_WIDTH), ctx_spec, tile(D_MODEL), _full_spec(wo.shape),
                  _full_spec(pw.shape), _full_spec(ps.shape), _full_spec(g.shape), _full_spec(b.shape)],
        out_specs=[tile(D_MODEL), ctx_spec],
        out_shape=[jax.ShapeDtypeStruct((nb, rows, D_MODEL), F32),
                   jax.ShapeDtypeStruct((nb, ctx_rows, POOL_WIDTH), F32)],
        scratch_shapes=[pltpu.VMEM((halo + tm, POOL_WIDTH), F32)] * 3,
        compiler_params=_params(2),
        name="sb_out",
    )(oa, u, pctx, x, wo, pw, ps, g, b)


def _ffn_kernel(x_ref, ctx_ref, wup_ref, cw_ref, cb_ref, wdn_ref, g_ref, b_ref,
                y_ref, nctx_ref, xb_ref, ext_ref, tail_ref, acc_ref, *, tm, ts, n_tiles, end_off):
    i = pl.program_id(1)
    ctx_rows = (FFN_CONV - 1) * ts
    halo = _round_up(ctx_rows, SUBLANES)

    @pl.when(i == 0)
    def _():
        if halo > ctx_rows:
            tail_ref[...] = jnp.zeros_like(tail_ref)
        for c in range(FFN_CHUNKS):
            tail_ref[c, halo - ctx_rows:halo, :] = ctx_ref[0, :, c * FFN_CHUNK:(c + 1) * FFN_CHUNK]

    xb_ref[...] = x_ref[0].astype(BF16)
    acc_ref[...] = jnp.zeros_like(acc_ref)

    def chunk(c, carry):
        xb = xb_ref[...]
        ext_ref[0:halo, :] = tail_ref[c]
        ext_ref[halo:halo + tm, :] = _dot(xb, wup_ref[0, c])
        val = _dot(xb, wup_ref[1, c])
        cw = cw_ref[c]
        gc = cb_ref[c]
        for k in range(FFN_CONV):
            off = halo - (FFN_CONV - 1 - k) * ts
            gc = gc + cw[k:k + 1, :] * ext_ref[off:off + tm, :]
        tail_ref[c] = ext_ref[tm:tm + halo, :]

        @pl.when(i == n_tiles - 1)
        def _():
            nctx_ref[0, c] = ext_ref[halo + end_off - ctx_rows:halo + end_off, :]

        hidden = (jax.nn.gelu(gc) * val).astype(BF16)
        acc_ref[...] += _dot(hidden, wdn_ref[c])
        return carry

    lax.fori_loop(0, FFN_CHUNKS, chunk, 0)
    y_ref[0] = _layer_norm(DN_ALPHA * x_ref[0] + acc_ref[...], g_ref[...], b_ref[...])


def _ffn_call(x, ctx, wup, cw, cb, wdn, g, b, *, tm, ts, end):
    nb, rows, _ = x.shape
    n_tiles = rows // tm
    assert n_tiles == 1 or ts == 1
    assert rows - tm < end <= rows
    ctx_rows = (FFN_CONV - 1) * ts
    halo = _round_up(ctx_rows, SUBLANES)
    tile = pl.BlockSpec((1, tm, D_MODEL), lambda bi, i: (bi, i, 0))
    return pl.pallas_call(
        functools.partial(_ffn_kernel, tm=tm, ts=ts, n_tiles=n_tiles, end_off=end - (rows - tm)),
        grid=(nb, n_tiles),
        in_specs=[tile, pl.BlockSpec((1, ctx_rows, FFN_DIM), lambda bi, i: (bi, 0, 0)),
                  _full_spec(wup.shape), _full_spec(cw.shape), _full_spec(cb.shape), _full_spec(wdn.shape),
                  _full_spec(g.shape), _full_spec(b.shape)],
        out_specs=[tile, pl.BlockSpec((1, FFN_CHUNKS, ctx_rows, FFN_CHUNK), lambda bi, i: (bi, 0, 0, 0))],
        out_shape=[jax.ShapeDtypeStruct((nb, rows, D_MODEL), F32),
                   jax.ShapeDtypeStruct((nb, FFN_CHUNKS, ctx_rows, FFN_CHUNK), F32)],
        scratch_shapes=[pltpu.VMEM((tm, D_MODEL), BF16), pltpu.VMEM((halo + tm, FFN_CHUNK), F32),
                        pltpu.VMEM((FFN_CHUNKS, halo, FFN_CHUNK), F32), pltpu.VMEM((tm, D_MODEL), F32)],
        compiler_params=_params(2),
        name="conv_ffn",
    )(x, ctx, wup, cw, cb, wdn, g, b)


def _lru_kernel(x_ref, cctx_ref, h0_ref, win_ref, cw_ref, cb_ref, wa_ref, ba_ref, wx_ref, bx_ref, lam_ref,
                wout_ref, g_ref, b_ref, y_ref, nctx_ref, nh_ref,
                ext_ref, a_ref, hb_ref, gate_ref, h_ref, *, tm, ts, n_tiles, end_off):
    i = pl.program_id(1)
    ctx_rows = (LRU_CONV - 1) * ts
    halo = _round_up(ctx_rows, SUBLANES)

    @pl.when(i == 0)
    def _():
        if halo > ctx_rows:
            ext_ref[0:halo - ctx_rows, :] = jnp.zeros((halo - ctx_rows, LRU_WIDTH), F32)
        ext_ref[halo - ctx_rows:halo, :] = cctx_ref[0]
        h_ref[...] = h0_ref[0]

    xb = x_ref[0].astype(BF16)
    gate_ref[...] = jax.nn.gelu(_dot(xb, win_ref[:, 0:LRU_WIDTH]))
    ext_ref[halo:halo + tm, :] = _dot(xb, win_ref[:, LRU_WIDTH:2 * LRU_WIDTH])

    lam = lam_ref[...]
    neg_c_softplus = -LRU_C * (jnp.maximum(-lam, 0.0) + jnp.log1p(jnp.exp(-jnp.abs(lam))))
    for n in range(LRU_BLOCKS):
        c = slice(n * LRU_BLOCK_DIM, (n + 1) * LRU_BLOCK_DIM)
        xc = cb_ref[:, c]
        for k in range(LRU_CONV):
            off = halo - (LRU_CONV - 1 - k) * ts
            xc = xc + cw_ref[k:k + 1, c] * ext_ref[off:off + tm, c]
        xcb = xc.astype(BF16)
        gate_r = jax.nn.sigmoid(_dot(xcb, wa_ref[n]) + ba_ref[:, c])
        gate_i = jax.nn.sigmoid(_dot(xcb, wx_ref[n]) + bx_ref[:, c])
        log_a = neg_c_softplus[:, c] * gate_r
        a_ref[:, c] = jnp.exp(log_a)
        t = jnp.tanh(-log_a)
        hb_ref[:, c] = jnp.sqrt(2.0 * t / (1.0 + t)) * gate_i * xc

    @pl.when(i == n_tiles - 1)
    def _():
        nctx_ref[0] = ext_ref[halo + end_off - ctx_rows:halo + end_off, :]

    if n_tiles > 1:
        ext_ref[0:halo, :] = ext_ref[tm:tm + halo, :]

    def step(t, h):
        r = pl.ds(pl.multiple_of(t * ts, ts), ts)
        hn = a_ref[r, :] * h + hb_ref[r, :]
        hb_ref[r, :] = hn
        return hn

    h_ref[...] = lax.fori_loop(0, tm // ts, step, h_ref[...], unroll=8 if tm // ts >= 8 else True)

    @pl.when(i == n_tiles - 1)
    def _():
        nh_ref[0] = hb_ref[end_off - ts:end_off, :]

    mixed = (gate_ref[...] * hb_ref[...]).astype(BF16)
    y_ref[0] = _layer_norm(DN_ALPHA * x_ref[0] + _dot(mixed, wout_ref[...]), g_ref[...], b_ref[...])


def _lru_call(x, cctx, h0, win, cw, cb, wa, ba, wx, bx, lam, wout, g, b, *, tm, ts, end):
    nb, rows, _ = x.shape
    n_tiles = rows // tm
    assert n_tiles == 1 or ts == 1
    assert rows - tm < end <= rows
    ctx_rows = (LRU_CONV - 1) * ts
    halo = _round_up(ctx_rows, SUBLANES)
    tile = pl.BlockSpec((1, tm, D_MODEL), lambda bi, i: (bi, i, 0))
    ctx_spec = pl.BlockSpec((1, ctx_rows, LRU_WIDTH), lambda bi, i: (bi, 0, 0))
    h_spec = pl.BlockSpec((1, ts, LRU_WIDTH), lambda bi, i: (bi, 0, 0))
    consts = [win, cw, cb, wa, ba, wx, bx, lam, wout, g, b]
    return pl.pallas_call(
        functools.partial(_lru_kernel, tm=tm, ts=ts, n_tiles=n_tiles, end_off=end - (rows - tm)),
        grid=(nb, n_tiles),
        in_specs=[tile, ctx_spec, h_spec] + [_full_spec(c.shape) for c in consts],
        out_specs=[tile, ctx_spec, h_spec],
        out_shape=[jax.ShapeDtypeStruct((nb, rows, D_MODEL), F32),
                   jax.ShapeDtypeStruct((nb, ctx_rows, LRU_WIDTH), F32),
                   jax.ShapeDtypeStruct((nb, ts, LRU_WIDTH), F32)],
        scratch_shapes=[pltpu.VMEM((halo + tm, LRU_WIDTH), F32), pltpu.VMEM((tm, LRU_WIDTH), F32),
                        pltpu.VMEM((tm, LRU_WIDTH), F32), pltpu.VMEM((tm, LRU_WIDTH), F32),
                        pltpu.VMEM((ts, LRU_WIDTH), F32)],
        compiler_params=_params(2),
        name="rg_lru",
    )(x, cctx, h0, *consts)


def _trunk(x, attend, states, w, *, tm, ts, end, pos0):
    nb, rows, _ = x.shape
    pool_ctx, lru_cctx, lru_h0, ffn_ctx0, ffn_ctx1 = states
    q, k, v, kb, vb, u = _sb_in_call(x.reshape(nb * rows, D_MODEL), w["sb_w_in"], tm)
    shp = (nb, rows, SB_WIDTH)
    oa = attend(q.reshape(shp), k.reshape(shp), v.reshape(shp), kb.reshape(shp), vb.reshape(shp))
    x, new_pool = _sb_out_call(oa, u.reshape(shp), pool_ctx, x, w["sb_w_out"], w["pool_w"], w["pool_scale"],
                               w["ln_g"][0], w["ln_b"][0], tm=tm, ts=ts, end=end, pos0=pos0)
    x, new_ffn0 = _ffn_call(x, ffn_ctx0, w["ffn_up"][0], w["ffn_cw"][0], w["ffn_cb"][0], w["ffn_dn"][0],
                            w["ln_g"][1], w["ln_b"][1], tm=tm, ts=ts, end=end)
    x, new_cctx, new_h = _lru_call(x, lru_cctx, lru_h0, w["lru_w_in"], w["lru_conv_w"], w["lru_conv_b"],
                                   w["lru_w_a"], w["lru_b_a"], w["lru_w_x"], w["lru_b_x"], w["lru_lambda"],
                                   w["lru_w_out"], w["ln_g"][2], w["ln_b"][2], tm=tm, ts=ts, end=end)
    x, new_ffn1 = _ffn_call(x, ffn_ctx1, w["ffn_up"][1], w["ffn_cw"][1], w["ffn_cb"][1], w["ffn_dn"][1],
                            w["ln_g"][3], w["ln_b"][3], tm=tm, ts=ts, end=end)
    return x, k.reshape(shp), v.reshape(shp), kb.reshape(shp), vb.reshape(shp), \
        (new_pool, new_cctx, new_h, new_ffn0, new_ffn1)


def _ffn_ctx_rows(c):
    nb, _, r, _ = c.shape
    return c.transpose(0, 2, 1, 3).reshape(nb, r, FFN_DIM)


def _to_time_major(s):
    db, t, c = s.shape
    return s.transpose(1, 0, 2).reshape(1, t * db, c)


def _from_time_major(s, db):
    _, r, c = s.shape
    return s.reshape(r // db, db, c).transpose(1, 0, 2)


META_TILE = 128
PROMPT_TILE = 512
ATTN_TQ = 256
ATTN_TK = 256


def kernel(x_prompt, x_sample, cache_sb_k, cache_sb_v, page_table, state_pool, state_lru_conv, state_lru_h,
           state_ffn_conv, meta_tokens, sb_w_in, sb_logit_bias, sb_w_out, pool_w, pool_scale, lru_w_in,
           lru_conv_w, lru_conv_b, lru_w_a, lru_b_a, lru_w_x, lru_b_x, lru_lambda, lru_w_out, ffn_w_up,
           ffn_conv_w, ffn_conv_b, ffn_w_down, ln_g, ln_b):
    bp, seq, _ = x_prompt.shape
    db, t_new, _ = x_sample.shape
    row = lambda a: a.reshape(1, -1).astype(F32)
    w = dict(
        sb_w_in=sb_w_in[0].astype(BF16), sb_w_out=sb_w_out[0].astype(BF16), pool_w=pool_w[0].astype(BF16),
        pool_scale=row(pool_scale[0]),
        lru_w_in=lru_w_in[0].astype(BF16), lru_conv_w=lru_conv_w[0].astype(F32), lru_conv_b=row(lru_conv_b[0]),
        lru_w_a=lru_w_a[0].astype(BF16), lru_b_a=row(lru_b_a[0]), lru_w_x=lru_w_x[0].astype(BF16),
        lru_b_x=row(lru_b_x[0]), lru_lambda=row(lru_lambda[0]), lru_w_out=lru_w_out[0].astype(BF16),
        ffn_up=[ffn_w_up[l].reshape(D_MODEL, 2, FFN_CHUNKS, FFN_CHUNK).transpose(1, 2, 0, 3).astype(BF16)
                for l in range(DEPTH)],
        ffn_dn=[ffn_w_down[l].reshape(FFN_CHUNKS, FFN_CHUNK, D_MODEL).astype(BF16) for l in range(DEPTH)],
        ffn_cw=[ffn_conv_w[l].reshape(FFN_CONV, FFN_CHUNKS, FFN_CHUNK).transpose(1, 0, 2).astype(F32)
                for l in range(DEPTH)],
        ffn_cb=[ffn_conv_b[l].reshape(FFN_CHUNKS, 1, FFN_CHUNK).astype(F32) for l in range(DEPTH)],
        ln_g=[row(ln_g[l, s]) for l in range(DEPTH) for s in range(2)],
        ln_b=[row(ln_b[l, s]) for l in range(DEPTH) for s in range(2)],
    )
    bias = sb_logit_bias[0].astype(F32)

    x_meta = jnp.concatenate([meta_tokens.astype(F32), jnp.zeros((META_TILE - N_META, D_MODEL), F32)])[None]
    zero_states = (jnp.zeros((1, POOL_CTX, POOL_WIDTH), F32), jnp.zeros((1, LRU_CONV - 1, LRU_WIDTH), F32),
                   jnp.zeros((1, 1, LRU_WIDTH), F32), jnp.zeros((1, FFN_CONV - 1, FFN_DIM), F32),
                   jnp.zeros((1, FFN_CONV - 1, FFN_DIM), F32))
    meta_attend = lambda q, k, v, kb, vb: _sb_attn_call(bias, q, kb, vb, None, META_TILE, META_TILE)
    _, k_meta, v_meta, kb_meta, vb_meta, st = _trunk(x_meta, meta_attend, zero_states, w, tm=META_TILE, ts=1,
                                                      end=N_META, pos0=0)

    per_seq = lambda s: jnp.broadcast_to(s, (bp,) + s.shape[1:])
    prompt_states = (per_seq(st[0]), per_seq(st[1]), per_seq(st[2]),
                     per_seq(_ffn_ctx_rows(st[3])), per_seq(_ffn_ctx_rows(st[4])))
    prompt_attend = lambda q, k, v, kb, vb: _sb_attn_call(bias, q, kb, vb, (kb_meta[0], vb_meta[0], N_META),
                                                          ATTN_TQ, ATTN_TK)
    y_prompt, k_real, v_real, _, _, sp = _trunk(x_prompt.astype(F32), prompt_attend, prompt_states, w,
                                                tm=PROMPT_TILE, ts=1, end=seq, pos0=N_META)

    n_pages = page_table.shape[1]
    cache_k = cache_sb_k[0].reshape(-1, PAGE_SIZE, SB_WIDTH)
    cache_v = cache_sb_v[0].reshape(-1, PAGE_SIZE, SB_WIDTH)

    def sample_attend(q, k, v, kb, vb):
        o = _sb_sample_call(page_table, _from_time_major(q.astype(F32), db), _from_time_major(k, db),
                            _from_time_major(v, db), bias, cache_k, cache_v)
        return _to_time_major(o).astype(BF16)

    sample_states = (_to_time_major(state_pool[0]), _to_time_major(state_lru_conv[0]), state_lru_h[0][None],
                     _to_time_major(state_ffn_conv[0]), _to_time_major(state_ffn_conv[1]))
    rows_s = db * t_new
    y_s, k_s, v_s, _, _, ss = _trunk(_to_time_major(x_sample.astype(F32)), sample_attend, sample_states, w,
                                     tm=rows_s, ts=db, end=rows_s, pos0=n_pages * PAGE_SIZE)

    heads = lambda a: a.reshape(a.shape[:-1] + (SB_HEADS, SB_HEAD_DIM))
    with_meta = lambda m, r: heads(jnp.concatenate([per_seq(m[:, :N_META]), r], axis=1))[None]
    return (
        y_prompt,
        _from_time_major(y_s, db),
        with_meta(k_meta, k_real),
        with_meta(v_meta, v_real),
        heads(_from_time_major(k_s, db))[None],
        heads(_from_time_major(v_s, db))[None],
        sp[0][None],
        _from_time_major(ss[0], db)[None],
        sp[1][None],
        _from_time_major(ss[1], db)[None],
        sp[2].reshape(1, bp, LRU_WIDTH),
        ss[2].reshape(1, db, LRU_WIDTH),
        jnp.stack([_ffn_ctx_rows(sp[3]), _ffn_ctx_rows(sp[4])]),
        jnp.stack([_from_time_major(_ffn_ctx_rows(ss[3]), db), _from_time_major(_ffn_ctx_rows(ss[4]), db)]),
    )
```

```python
import functools

import jax
import jax.numpy as jnp
from jax import lax
from jax.experimental import pallas as pl
from jax.experimental.pallas import tpu as pltpu

F32 = jnp.float32
BF16 = jnp.bfloat16

D_MODEL = 1024
N_META = 16
SB_HEADS = 8
SB_HEAD_DIM = 64
SB_WIDTH = SB_HEADS * SB_HEAD_DIM
HEAD_PAIRS = SB_HEADS // 2
POOL_WINDOWS = (2, 4, 8, 16)
POOL_WIDTH = 512
POOL_GROUP_DIM = 128
POOL_CTX = 15
LRU_WIDTH = 1024
LRU_BLOCKS = 8
LRU_BLOCK_DIM = 128
LRU_CONV = 4
LRU_C = 8.0
FFN_DIM = 2816
FFN_CONV = 3
FFN_CHUNK = 256
FFN_CHUNKS = FFN_DIM // FFN_CHUNK
LN_EPS = 1e-5
DEPTH = 2
DN_ALPHA = (2.0 * DEPTH) ** 0.25
PAGE_SIZE = 128

LANES = 128
SUBLANES = 8
VMEM_LIMIT = 56 * 1024 * 1024
LOG2E = 1.4426950408889634
NEG_BIG = -1e30


def _round_up(x, m):
    return (x + m - 1) // m * m


def _params(n_axes):
    return pltpu.CompilerParams(dimension_semantics=("arbitrary",) * n_axes,
                                vmem_limit_bytes=VMEM_LIMIT)


def _full_spec(shape):
    nd = len(shape)
    return pl.BlockSpec(shape, lambda *_: (0,) * nd)


def _layer_norm(y, g, b):
    mu = jnp.mean(y, axis=-1, keepdims=True)
    yc = y - mu
    var = jnp.mean(yc * yc, axis=-1, keepdims=True)
    return yc * lax.rsqrt(var + LN_EPS) * g + b


def _neg_abs(z):
    bits = lax.bitcast_convert_type(z, jnp.uint32) | jnp.uint32(0x80000000)
    return lax.bitcast_convert_type(bits, F32)


def _softplus2(z):
    return jnp.maximum(z, 0.0) + jnp.log(1.0 + jnp.exp2(_neg_abs(z))) * LOG2E


def _dot(a, b):
    return jnp.dot(a, b, preferred_element_type=F32)


def _dot_nt(a, b):
    return lax.dot_general(a, b, (((1,), (1,)), ((), ())), preferred_element_type=F32)


def _neg_upper(n):
    j = lax.broadcasted_iota(jnp.int32, (n, n), 0)
    s = lax.broadcasted_iota(jnp.int32, (n, n), 1)
    return jnp.where(j >= s, -1.0, 0.0).astype(BF16)


def _sb_in_kernel(x_ref, w_ref, kb_init_ref, vb_init_ref, q_ref, k_ref, v_ref, kb_ref, vb_ref, u_ref):
    del kb_init_ref, vb_init_ref
    xb = x_ref[0].astype(BF16)
    q = _dot(xb, w_ref[:, 0:SB_WIDTH])
    q_ref[0] = (q * (LOG2E * SB_HEAD_DIM ** -0.5)).astype(BF16)
    k = _dot(xb, w_ref[:, SB_WIDTH:2 * SB_WIDTH])
    k_ref[0] = k
    kb_ref[0] = k.astype(BF16)
    v = _dot(xb, w_ref[:, 2 * SB_WIDTH:3 * SB_WIDTH])
    v_ref[0] = v
    vb_ref[0] = v.astype(BF16)
    u_ref[0] = _dot(xb, w_ref[:, 3 * SB_WIDTH:3 * SB_WIDTH + POOL_WIDTH])


def _sb_in_call(x, w_in, tm, past_k, past_v):
    nb, rows, _ = x.shape
    past = past_k.shape[0]
    assert past % tm == 0
    behind = lambda p: jnp.concatenate([jnp.broadcast_to(p[None], (nb, past, SB_WIDTH)),
                                        jnp.zeros((nb, rows, SB_WIDTH), BF16)], axis=1)
    tile = lambda c: pl.BlockSpec((1, tm, c), lambda b, i: (b, i, 0))
    behind_past = pl.BlockSpec((1, tm, SB_WIDTH), lambda b, i: (b, i + past // tm, 0))
    any_spec = pl.BlockSpec(memory_space=pl.ANY)
    return pl.pallas_call(
        _sb_in_kernel,
        grid=(nb, rows // tm),
        in_specs=[tile(D_MODEL), _full_spec(w_in.shape), any_spec, any_spec],
        out_specs=[tile(SB_WIDTH), tile(SB_WIDTH), tile(SB_WIDTH), behind_past, behind_past, tile(POOL_WIDTH)],
        out_shape=[jax.ShapeDtypeStruct((nb, rows, SB_WIDTH), BF16),
                   jax.ShapeDtypeStruct((nb, rows, SB_WIDTH), F32),
                   jax.ShapeDtypeStruct((nb, rows, SB_WIDTH), F32),
                   jax.ShapeDtypeStruct((nb, past + rows, SB_WIDTH), BF16),
                   jax.ShapeDtypeStruct((nb, past + rows, SB_WIDTH), BF16),
                   jax.ShapeDtypeStruct((nb, rows, POOL_WIDTH), F32)],
        input_output_aliases={2: 3, 3: 4},
        compiler_params=_params(2),
        name="sb_in",
    )(x, w_in, behind(past_k), behind(past_v))


def _sb_attn_kernel(bias_ref, q_ref, k_ref, v_ref, un_ref, o_ref, acc_ref, car_ref, sp_ref, zc_ref, *,
                    tq, tk, n_past):
    hp = pl.program_id(1)
    qi = pl.program_id(2)
    n_diag = tq // tk
    lane = lax.broadcasted_iota(jnp.int32, (tq, LANES), 1)
    q = q_ref[0]
    zero = jnp.zeros_like(q)
    qh = (jnp.where(lane < SB_HEAD_DIM, q, zero), jnp.where(lane >= SB_HEAD_DIM, q, zero))
    bias = (bias_ref[2 * hp], bias_ref[2 * hp + 1])
    un = un_ref[...]
    row = lax.broadcasted_iota(jnp.int32, (tq, tk), 0)
    col = lax.broadcasted_iota(jnp.int32, (tq, tk), 1)

    acc_ref[...] = jnp.zeros_like(acc_ref)
    car_ref[...] = jnp.zeros_like(car_ref)

    def stage_a(t, par, mask):
        kt = k_ref[0, pl.ds(pl.multiple_of(t * tk, tk), tk), :]
        for h in range(2):
            z = _dot_nt(qh[h], kt) + bias[h]
            if mask is not None:
                z = jnp.where(mask, z, NEG_BIG)
            sp = _softplus2(z)
            sp_ref[par, h] = sp.astype(BF16)
            car = car_ref[h]
            zc_ref[par, h] = z + jnp.concatenate([car] * (tk // LANES), axis=1)
            car_ref[h] = car - jnp.broadcast_to(jnp.sum(sp, axis=-1, keepdims=True), (tq, LANES))

    def stage_b(t, par):
        vt = v_ref[0, pl.ds(pl.multiple_of(t * tk, tk), tk), :]
        for h in range(2):
            incl = _dot(sp_ref[par, h], un)
            a = jnp.exp2(zc_ref[par, h] + incl)
            acc_ref[h] += _dot(a.astype(BF16), vt)

    base = qi * n_diag
    prev = None
    for d in reversed(range(n_diag)):
        t, par = base + 1 + d, (1 + d) % 2
        stage_a(t, par, col + d * tk < row)
        if prev is not None:
            stage_b(*prev)
        prev = (t, par)

    def body(jj, carry):
        te = base - 2 * jj
        stage_a(te, 0, None)
        stage_b(te + 1, 1)
        stage_a(te - 1, 1, None)
        stage_b(te, 0)
        return carry

    lax.fori_loop(0, base // 2, body, 0)

    stage_a(0, 0, col < n_past)
    stage_b(1, 1)
    stage_b(0, 0)

    o_ref[0] = jnp.where(lane < SB_HEAD_DIM, acc_ref[0], acc_ref[1]).astype(BF16)


def _sb_attn_call(bias, q, k_all, v_all, n_past, tq, tk):
    nb, rows, _ = q.shape
    n_diag = tq // tk
    assert n_diag % 2 == 0 or rows == tq == tk
    tile_spec = pl.BlockSpec((1, tq, LANES), lambda b, h, i: (b, i, h))
    seq_spec = pl.BlockSpec((1, tk + rows, LANES), lambda b, h, i: (b, 0, h))
    return pl.pallas_call(
        functools.partial(_sb_attn_kernel, tq=tq, tk=tk, n_past=n_past),
        grid=(nb, HEAD_PAIRS, rows // tq),
        in_specs=[pl.BlockSpec(memory_space=pltpu.SMEM), tile_spec, seq_spec, seq_spec, _full_spec((tk, tk))],
        out_specs=tile_spec,
        out_shape=jax.ShapeDtypeStruct((nb, rows, SB_WIDTH), BF16),
        scratch_shapes=[pltpu.VMEM((2, tq, LANES), F32), pltpu.VMEM((2, tq, LANES), F32),
                        pltpu.VMEM((2, 2, tq, tk), BF16), pltpu.VMEM((2, 2, tq, tk), F32)],
        compiler_params=_params(3),
        name="sb_attn",
    )(bias, q, k_all, v_all, _neg_upper(tk))


PAGES_PER_STEP = 8
PAGE_ROWS = PAGE_SIZE * SB_HEADS


def _sb_sample_kernel(pt_ref, q_ref, kn_ref, vn_ref, bias_ref, un_ref, *refs, n_steps, t_new):
    del pt_ref
    page_refs = refs[:2 * PAGES_PER_STEP]
    o_ref, new_ref, acc_ref, car_ref = refs[2 * PAGES_PER_STEP:]
    j = pl.program_id(1)
    qrows = SB_HEADS * t_new
    bias = bias_ref[...]
    un = un_ref[...]
    qh = [q_ref[0, h].astype(BF16) for h in range(SB_HEADS)]

    def head_rows(ref, lead, h):
        return ref[lead, pl.ds(h, PAGE_SIZE, stride=SB_HEADS), :].astype(BF16)

    def visit(pages, mask):
        zs = []
        for k_ref, k_lead, _, _ in pages:
            z = jnp.concatenate([_dot_nt(qh[h], head_rows(k_ref, k_lead, h)) for h in range(SB_HEADS)], axis=0)
            z = z + bias
            if mask is not None:
                z = jnp.where(mask, z, NEG_BIG)
            zs.append(z)
        sps = [_softplus2(z) for z in zs]
        incl = _dot(jnp.concatenate(sps, axis=0).astype(BF16), un)
        car = car_ref[...]
        pvs = [None] * SB_HEADS
        for p, (_, _, v_ref, v_lead) in enumerate(pages):
            a = jnp.exp2(zs[p] + incl[p * qrows:(p + 1) * qrows, :] + car)
            car = car - jnp.broadcast_to(jnp.sum(sps[p], axis=-1, keepdims=True), (qrows, LANES))
            for h in range(SB_HEADS):
                ah = a[h * t_new:(h + 1) * t_new, :].astype(BF16)
                pv = _dot(ah, head_rows(v_ref, v_lead, h))
                pvs[h] = pv if pvs[h] is None else pvs[h] + pv
        car_ref[...] = car
        for h in range(SB_HEADS):
            acc_ref[h] += pvs[h]

    @pl.when(j == 0)
    def _():
        acc_ref[...] = jnp.zeros_like(acc_ref)
        car_ref[...] = jnp.zeros_like(car_ref)
        new_ref[...] = jnp.zeros_like(new_ref)
        new_ref[0, 0:qrows, :] = kn_ref[0]
        new_ref[1, 0:qrows, :] = vn_ref[0]
        t_of_row = lax.broadcasted_iota(jnp.int32, (qrows, PAGE_SIZE), 0) % t_new
        col = lax.broadcasted_iota(jnp.int32, (qrows, PAGE_SIZE), 1)
        visit([(new_ref, 0, new_ref, 1)], col < t_of_row)

    visit([(page_refs[p], 0, page_refs[PAGES_PER_STEP + p], 0) for p in range(PAGES_PER_STEP)], None)

    @pl.when(j == n_steps - 1)
    def _():
        o_ref[0] = acc_ref[...]


def _sb_sample_call(page_table, q, kn, vn, bias, cache_k, cache_v):
    nb, _, t_new, _ = q.shape
    qrows = SB_HEADS * t_new
    n_pages = page_table.shape[1]
    n_steps = n_pages // PAGES_PER_STEP
    q_spec = pl.BlockSpec((1, SB_HEADS, t_new, SB_HEAD_DIM), lambda b, j, pt: (b, 0, 0, 0))
    new_spec = pl.BlockSpec((1, qrows, SB_HEAD_DIM), lambda b, j, pt: (b, 0, 0))

    def page_spec(p):
        return pl.BlockSpec((1, PAGE_ROWS, SB_HEAD_DIM),
                            lambda b, j, pt: (pt[b, n_pages - 1 - (j * PAGES_PER_STEP + p)], 0, 0))

    page_specs = [page_spec(p) for p in range(PAGES_PER_STEP)]
    bias_rows = jnp.repeat(bias, t_new)[:, None]
    grid_spec = pltpu.PrefetchScalarGridSpec(
        num_scalar_prefetch=1,
        grid=(nb, n_steps),
        in_specs=[q_spec, new_spec, new_spec,
                  pl.BlockSpec((qrows, 1), lambda b, j, pt: (0, 0)),
                  pl.BlockSpec((PAGE_SIZE, PAGE_SIZE), lambda b, j, pt: (0, 0))] + page_specs + page_specs,
        out_specs=q_spec,
        scratch_shapes=[pltpu.VMEM((2, PAGE_ROWS, SB_HEAD_DIM), F32),
                        pltpu.VMEM((SB_HEADS, t_new, SB_HEAD_DIM), F32), pltpu.VMEM((qrows, LANES), F32)],
    )
    return pl.pallas_call(
        functools.partial(_sb_sample_kernel, n_steps=n_steps, t_new=t_new),
        grid_spec=grid_spec,
        out_shape=jax.ShapeDtypeStruct((nb, SB_HEADS, t_new, SB_HEAD_DIM), F32),
        compiler_params=_params(2),
        name="sb_sample",
    )(page_table, q, kn, vn, bias_rows, _neg_upper(PAGE_SIZE),
      *([cache_k] * PAGES_PER_STEP), *([cache_v] * PAGES_PER_STEP))


def _pool_layout(ts):
    if ts == 1:
        return 32, (8, 16, 24, 32)
    halo = POOL_CTX * ts
    return halo, (halo - 14 * ts, halo - 12 * ts, halo - 8 * ts, halo)


def _sb_out_kernel(oa_ref, u_ref, pctx_ref, x_ref, wo_ref, pw_ref, ps_ref, g_ref, b_ref,
                   y_ref, nctx_ref, ext_ref, pa_ref, pb_ref, *, tm, ts, n_tiles, end_off, pos0):
    i = pl.program_id(1)
    halo, starts = _pool_layout(ts)
    rows = halo + tm
    ctx_rows = POOL_CTX * ts

    @pl.when(i == 0)
    def _():
        if halo > ctx_rows:
            ext_ref[0:halo - ctx_rows, :] = jnp.zeros((halo - ctx_rows, POOL_WIDTH), F32)
        ext_ref[halo - ctx_rows:halo, :] = pctx_ref[0]

    u = u_ref[0]
    ext_ref[halo:rows, :] = u

    src = ext_ref
    bufs = (pa_ref, pb_ref, pa_ref, pb_ref)
    for s in range(4):
        lo = starts[s]
        shift = (1 << s) * ts
        c0 = s * POOL_GROUP_DIM
        dst = bufs[s]
        dst[lo:rows, c0:] = src[lo:rows, c0:] + src[lo - shift:rows - shift, c0:]
        src = dst

    if pos0 + 1 >= POOL_WINDOWS[-1]:
        inv = [1.0 / w for w in POOL_WINDOWS]
    else:
        pos = pos0 + i * (tm // ts) + lax.broadcasted_iota(jnp.int32, (tm, POOL_GROUP_DIM), 0) // ts
        inv = [1.0 / jnp.minimum(pos + 1, w).astype(F32) for w in POOL_WINDOWS]

    out = _dot(oa_ref[0], wo_ref[0:SB_WIDTH, :])
    for g in range(4):
        c = slice(g * POOL_GROUP_DIM, (g + 1) * POOL_GROUP_DIM)
        d = bufs[g][halo:rows, c] * inv[g] - u[:, c]
        mixed = _dot(d.astype(BF16), pw_ref[g]) * ps_ref[:, c]
        out = out + _dot(mixed.astype(BF16), wo_ref[SB_WIDTH + g * POOL_GROUP_DIM:SB_WIDTH + (g + 1) * POOL_GROUP_DIM, :])

    y_ref[0] = _layer_norm(DN_ALPHA * x_ref[0] + out, g_ref[...], b_ref[...])

    @pl.when(i == n_tiles - 1)
    def _():
        nctx_ref[0] = ext_ref[halo + end_off - ctx_rows:halo + end_off, :]

    if n_tiles > 1:
        ext_ref[halo - 16:halo, :] = ext_ref[rows - 16:rows, :]


def _sb_out_call(oa, u, pctx, x, wo, pw, ps, g, b, *, tm, ts, end, pos0):
    nb, rows, _ = x.shape
    n_tiles = rows // tm
    assert n_tiles == 1 or ts == 1
    assert rows - tm < end <= rows
    halo, _ = _pool_layout(ts)
    ctx_rows = POOL_CTX * ts
    tile = lambda c: pl.BlockSpec((1, tm, c), lambda bi, i: (bi, i, 0))
    ctx_spec = pl.BlockSpec((1, ctx_rows, POOL_WIDTH), lambda bi, i: (bi, 0, 0))
    return pl.pallas_call(
        functools.partial(_sb_out_kernel, tm=tm, ts=ts, n_tiles=n_tiles, end_off=end - (rows - tm), pos0=pos0),
        grid=(nb, n_tiles),
        in_specs=[tile(SB_WIDTH), tile(POOL_WIDTH), ctx_spec, tile(D_MODEL), _full_spec(wo.shape),
                  _full_spec(pw.shape), _full_spec(ps.shape), _full_spec(g.shape), _full_spec(b.shape)],
        out_specs=[tile(D_MODEL), ctx_spec],
        out_shape=[jax.ShapeDtypeStruct((nb, rows, D_MODEL), F32),
                   jax.ShapeDtypeStruct((nb, ctx_rows, POOL_WIDTH), F32)],
        scratch_shapes=[pltpu.VMEM((halo + tm, POOL_WIDTH), F32)] * 3,
        compiler_params=_params(2),
        name="sb_out",
    )(oa, u, pctx, x, wo, pw, ps, g, b)


def _ffn_kernel(x_ref, ctx_ref, wup_ref, cw_ref, cb_ref, wdn_ref, g_ref, b_ref,
                y_ref, nctx_ref, xb_ref, ext_ref, tail_ref, acc_ref, *, tm, ts, n_tiles, end_off):
    i = pl.program_id(1)
    ctx_rows = (FFN_CONV - 1) * ts
    halo = _round_up(ctx_rows, SUBLANES)

    @pl.when(i == 0)
    def _():
        if halo > ctx_rows:
            tail_ref[...] = jnp.zeros_like(tail_ref)
        for c in range(FFN_CHUNKS):
            tail_ref[c, halo - ctx_rows:halo, :] = ctx_ref[0, :, c * FFN_CHUNK:(c + 1) * FFN_CHUNK]

    xb_ref[...] = x_ref[0].astype(BF16)
    acc_ref[...] = jnp.zeros_like(acc_ref)

    def chunk(c, carry):
        xb = xb_ref[...]
        ext_ref[0:halo, :] = tail_ref[c]
        ext_ref[halo:halo + tm, :] = _dot(xb, wup_ref[0, c])
        val = _dot(xb, wup_ref[1, c])
        cw = cw_ref[c]
        gc = cb_ref[c]
        for k in range(FFN_CONV):
            off = halo - (FFN_CONV - 1 - k) * ts
            gc = gc + cw[k:k + 1, :] * ext_ref[off:off + tm, :]
        tail_ref[c] = ext_ref[tm:tm + halo, :]

        @pl.when(i == n_tiles - 1)
        def _():
            nctx_ref[0, c] = ext_ref[halo + end_off - ctx_rows:halo + end_off, :]

        hidden = (jax.nn.gelu(gc) * val).astype(BF16)
        acc_ref[...] += _dot(hidden, wdn_ref[c])
        return carry

    lax.fori_loop(0, FFN_CHUNKS, chunk, 0)
    y_ref[0] = _layer_norm(DN_ALPHA * x_ref[0] + acc_ref[...], g_ref[...], b_ref[...])


def _ffn_call(x, ctx, wup, cw, cb, wdn, g, b, *, tm, ts, end):
    nb, rows, _ = x.shape
    n_tiles = rows // tm
    assert n_tiles == 1 or ts == 1
    assert rows - tm < end <= rows
    ctx_rows = (FFN_CONV - 1) * ts
    halo = _round_up(ctx_rows, SUBLANES)
    tile = pl.BlockSpec((1, tm, D_MODEL), lambda bi, i: (bi, i, 0))
    return pl.pallas_call(
        functools.partial(_ffn_kernel, tm=tm, ts=ts, n_tiles=n_tiles, end_off=end - (rows - tm)),
        grid=(nb, n_tiles),
        in_specs=[tile, pl.BlockSpec((1, ctx_rows, FFN_DIM), lambda bi, i: (bi, 0, 0)),
                  _full_spec(wup.shape), _full_spec(cw.shape), _full_spec(cb.shape), _full_spec(wdn.shape),
                  _full_spec(g.shape), _full_spec(b.shape)],
        out_specs=[tile, pl.BlockSpec((1, FFN_CHUNKS, ctx_rows, FFN_CHUNK), lambda bi, i: (bi, 0, 0, 0))],
        out_shape=[jax.ShapeDtypeStruct((nb, rows, D_MODEL), F32),
                   jax.ShapeDtypeStruct((nb, FFN_CHUNKS, ctx_rows, FFN_CHUNK), F32)],
        scratch_shapes=[pltpu.VMEM((tm, D_MODEL), BF16), pltpu.VMEM((halo + tm, FFN_CHUNK), F32),
                        pltpu.VMEM((FFN_CHUNKS, halo, FFN_CHUNK), F32), pltpu.VMEM((tm, D_MODEL), F32)],
        compiler_params=_params(2),
        name="conv_ffn",
    )(x, ctx, wup, cw, cb, wdn, g, b)


def _lru_kernel(x_ref, cctx_ref, h0_ref, win_ref, cw_ref, cb_ref, wa_ref, ba_ref, wx_ref, bx_ref, lam_ref,
                wout_ref, g_ref, b_ref, y_ref, nctx_ref, nh_ref,
                ext_ref, a_ref, hb_ref, gate_ref, h_ref, *, tm, ts, n_tiles, end_off):
    i = pl.program_id(1)
    ctx_rows = (LRU_CONV - 1) * ts
    halo = _round_up(ctx_rows, SUBLANES)

    @pl.when(i == 0)
    def _():
        if halo > ctx_rows:
            ext_ref[0:halo - ctx_rows, :] = jnp.zeros((halo - ctx_rows, LRU_WIDTH), F32)
        ext_ref[halo - ctx_rows:halo, :] = cctx_ref[0]
        h_ref[...] = h0_ref[0]

    xb = x_ref[0].astype(BF16)
    gate_ref[...] = jax.nn.gelu(_dot(xb, win_ref[:, 0:LRU_WIDTH]))
    ext_ref[halo:halo + tm, :] = _dot(xb, win_ref[:, LRU_WIDTH:2 * LRU_WIDTH])

    lam = lam_ref[...]
    neg_c_softplus = -LRU_C * (jnp.maximum(-lam, 0.0) + jnp.log1p(jnp.exp(-jnp.abs(lam))))
    for n in range(LRU_BLOCKS):
        c = slice(n * LRU_BLOCK_DIM, (n + 1) * LRU_BLOCK_DIM)
        xc = cb_ref[:, c]
        for k in range(LRU_CONV):
            off = halo - (LRU_CONV - 1 - k) * ts
            xc = xc + cw_ref[k:k + 1, c] * ext_ref[off:off + tm, c]
        xcb = xc.astype(BF16)
        gate_r = jax.nn.sigmoid(_dot(xcb, wa_ref[n]) + ba_ref[:, c])
        gate_i = jax.nn.sigmoid(_dot(xcb, wx_ref[n]) + bx_ref[:, c])
        log_a = neg_c_softplus[:, c] * gate_r
        a_ref[:, c] = jnp.exp(log_a)
        t = jnp.tanh(-log_a)
        hb_ref[:, c] = jnp.sqrt(2.0 * t / (1.0 + t)) * gate_i * xc

    @pl.when(i == n_tiles - 1)
    def _():
        nctx_ref[0] = ext_ref[halo + end_off - ctx_rows:halo + end_off, :]

    if n_tiles > 1:
        ext_ref[0:halo, :] = ext_ref[tm:tm + halo, :]

    def step(t, h):
        r = pl.ds(pl.multiple_of(t * ts, ts), ts)
        hn = a_ref[r, :] * h + hb_ref[r, :]
        hb_ref[r, :] = hn
        return hn

    h_ref[...] = lax.fori_loop(0, tm // ts, step, h_ref[...], unroll=8 if tm // ts >= 8 else True)

    @pl.when(i == n_tiles - 1)
    def _():
        nh_ref[0] = hb_ref[end_off - ts:end_off, :]

    mixed = (gate_ref[...] * hb_ref[...]).astype(BF16)
    y_ref[0] = _layer_norm(DN_ALPHA * x_ref[0] + _dot(mixed, wout_ref[...]), g_ref[...], b_ref[...])


def _lru_call(x, cctx, h0, win, cw, cb, wa, ba, wx, bx, lam, wout, g, b, *, tm, ts, end):
    nb, rows, _ = x.shape
    n_tiles = rows // tm
    assert n_tiles == 1 or ts == 1
    assert rows - tm < end <= rows
    ctx_rows = (LRU_CONV - 1) * ts
    halo = _round_up(ctx_rows, SUBLANES)
    tile = pl.BlockSpec((1, tm, D_MODEL), lambda bi, i: (bi, i, 0))
    ctx_spec = pl.BlockSpec((1, ctx_rows, LRU_WIDTH), lambda bi, i: (bi, 0, 0))
    h_spec = pl.BlockSpec((1, ts, LRU_WIDTH), lambda bi, i: (bi, 0, 0))
    consts = [win, cw, cb, wa, ba, wx, bx, lam, wout, g, b]
    return pl.pallas_call(
        functools.partial(_lru_kernel, tm=tm, ts=ts, n_tiles=n_tiles, end_off=end - (rows - tm)),
        grid=(nb, n_tiles),
        in_specs=[tile, ctx_spec, h_spec] + [_full_spec(c.shape) for c in consts],
        out_specs=[tile, ctx_spec, h_spec],
        out_shape=[jax.ShapeDtypeStruct((nb, rows, D_MODEL), F32),
                   jax.ShapeDtypeStruct((nb, ctx_rows, LRU_WIDTH), F32),
                   jax.ShapeDtypeStruct((nb, ts, LRU_WIDTH), F32)],
        scratch_shapes=[pltpu.VMEM((halo + tm, LRU_WIDTH), F32), pltpu.VMEM((tm, LRU_WIDTH), F32),
                        pltpu.VMEM((tm, LRU_WIDTH), F32), pltpu.VMEM((tm, LRU_WIDTH), F32),
                        pltpu.VMEM((ts, LRU_WIDTH), F32)],
        compiler_params=_params(2),
        name="rg_lru",
    )(x, cctx, h0, *consts)


def _trunk(x, attend, states, w, *, tm, in_tile, past_kv, ts, end, pos0):
    pool_ctx, lru_cctx, lru_h0, ffn_ctx0, ffn_ctx1 = states
    q, k, v, kb, vb, u = _sb_in_call(x, w["sb_w_in"], in_tile, *past_kv)
    oa = attend(q, k, v, kb, vb)
    x, new_pool = _sb_out_call(oa, u, pool_ctx, x, w["sb_w_out"], w["pool_w"], w["pool_scale"],
                               w["ln_g"][0], w["ln_b"][0], tm=tm, ts=ts, end=end, pos0=pos0)
    x, new_ffn0 = _ffn_call(x, ffn_ctx0, w["ffn_up"][0], w["ffn_cw"][0], w["ffn_cb"][0], w["ffn_dn"][0],
                            w["ln_g"][1], w["ln_b"][1], tm=tm, ts=ts, end=end)
    x, new_cctx, new_h = _lru_call(x, lru_cctx, lru_h0, w["lru_w_in"], w["lru_conv_w"], w["lru_conv_b"],
                                   w["lru_w_a"], w["lru_b_a"], w["lru_w_x"], w["lru_b_x"], w["lru_lambda"],
                                   w["lru_w_out"], w["ln_g"][2], w["ln_b"][2], tm=tm, ts=ts, end=end)
    x, new_ffn1 = _ffn_call(x, ffn_ctx1, w["ffn_up"][1], w["ffn_cw"][1], w["ffn_cb"][1], w["ffn_dn"][1],
                            w["ln_g"][3], w["ln_b"][3], tm=tm, ts=ts, end=end)
    return x, k, v, kb, vb, (new_pool, new_cctx, new_h, new_ffn0, new_ffn1)


def _ffn_ctx_rows(c):
    nb, _, r, _ = c.shape
    return c.transpose(0, 2, 1, 3).reshape(nb, r, FFN_DIM)


def _to_time_major(s):
    db, t, c = s.shape
    return s.transpose(1, 0, 2).reshape(1, t * db, c)


def _from_time_major(s, db):
    _, r, c = s.shape
    return s.reshape(r // db, db, c).transpose(1, 0, 2)


META_TILE = 128
PROMPT_TILE = 512
ATTN_TQ = 512
ATTN_TK = 256


def kernel(x_prompt, x_sample, cache_sb_k, cache_sb_v, page_table, state_pool, state_lru_conv, state_lru_h,
           state_ffn_conv, meta_tokens, sb_w_in, sb_logit_bias, sb_w_out, pool_w, pool_scale, lru_w_in,
           lru_conv_w, lru_conv_b, lru_w_a, lru_b_a, lru_w_x, lru_b_x, lru_lambda, lru_w_out, ffn_w_up,
           ffn_conv_w, ffn_conv_b, ffn_w_down, ln_g, ln_b):
    bp, seq, _ = x_prompt.shape
    db, t_new, _ = x_sample.shape
    row = lambda a: a.reshape(1, -1).astype(F32)
    w = dict(
        sb_w_in=sb_w_in[0].astype(BF16), sb_w_out=sb_w_out[0].astype(BF16), pool_w=pool_w[0].astype(BF16),
        pool_scale=row(pool_scale[0]),
        lru_w_in=lru_w_in[0].astype(BF16), lru_conv_w=lru_conv_w[0].astype(F32), lru_conv_b=row(lru_conv_b[0]),
        lru_w_a=lru_w_a[0].astype(BF16), lru_b_a=row(lru_b_a[0]), lru_w_x=lru_w_x[0].astype(BF16),
        lru_b_x=row(lru_b_x[0]), lru_lambda=row(lru_lambda[0]), lru_w_out=lru_w_out[0].astype(BF16),
        ffn_up=[ffn_w_up[l].reshape(D_MODEL, 2, FFN_CHUNKS, FFN_CHUNK).transpose(1, 2, 0, 3).astype(BF16)
                for l in range(DEPTH)],
        ffn_dn=[ffn_w_down[l].reshape(FFN_CHUNKS, FFN_CHUNK, D_MODEL).astype(BF16) for l in range(DEPTH)],
        ffn_cw=[ffn_conv_w[l].reshape(FFN_CONV, FFN_CHUNKS, FFN_CHUNK).transpose(1, 0, 2).astype(F32)
                for l in range(DEPTH)],
        ffn_cb=[ffn_conv_b[l].reshape(FFN_CHUNKS, 1, FFN_CHUNK).astype(F32) for l in range(DEPTH)],
        ln_g=[row(ln_g[l, s]) for l in range(DEPTH) for s in range(2)],
        ln_b=[row(ln_b[l, s]) for l in range(DEPTH) for s in range(2)],
    )
    bias2 = sb_logit_bias[0].astype(F32) * LOG2E

    x_meta = jnp.concatenate([meta_tokens.astype(F32), jnp.zeros((META_TILE - N_META, D_MODEL), F32)])[None]
    zero_states = (jnp.zeros((1, POOL_CTX, POOL_WIDTH), F32), jnp.zeros((1, LRU_CONV - 1, LRU_WIDTH), F32),
                   jnp.zeros((1, 1, LRU_WIDTH), F32), jnp.zeros((1, FFN_CONV - 1, FFN_DIM), F32),
                   jnp.zeros((1, FFN_CONV - 1, FFN_DIM), F32))
    no_past = jnp.zeros((META_TILE, SB_WIDTH), BF16)
    meta_attend = lambda q, k, v, kb, vb: _sb_attn_call(bias2, q, kb, vb, 0, META_TILE, META_TILE)
    _, k_meta, v_meta, kb_meta, vb_meta, st = _trunk(x_meta, meta_attend, zero_states, w, tm=META_TILE,
                                                      in_tile=META_TILE, past_kv=(no_past, no_past), ts=1,
                                                      end=N_META, pos0=0)

    per_seq = lambda s: jnp.broadcast_to(s, (bp,) + s.shape[1:])
    prompt_states = (per_seq(st[0]), per_seq(st[1]), per_seq(st[2]),
                     per_seq(_ffn_ctx_rows(st[3])), per_seq(_ffn_ctx_rows(st[4])))
    meta_block = lambda a: jnp.concatenate([a[0, META_TILE:], jnp.zeros((ATTN_TK - META_TILE, SB_WIDTH), BF16)])
    prompt_attend = lambda q, k, v, kb, vb: _sb_attn_call(bias2, q, kb, vb, N_META, ATTN_TQ, ATTN_TK)
    y_prompt, k_real, v_real, _, _, sp = _trunk(x_prompt.astype(F32), prompt_attend, prompt_states, w,
                                                tm=PROMPT_TILE, in_tile=ATTN_TK,
                                                past_kv=(meta_block(kb_meta), meta_block(vb_meta)), ts=1, end=seq,
                                                pos0=N_META)

    n_pages = page_table.shape[1]
    cache_k = cache_sb_k[0].reshape(-1, PAGE_ROWS, SB_HEAD_DIM)
    cache_v = cache_sb_v[0].reshape(-1, PAGE_ROWS, SB_HEAD_DIM)

    def sample_attend(q, k, v, kb, vb):
        split = lambda a: a.astype(F32).reshape(t_new, db, SB_HEADS, SB_HEAD_DIM)
        new_rows = lambda a: split(a).transpose(1, 0, 2, 3).reshape(db, t_new * SB_HEADS, SB_HEAD_DIM)
        o = _sb_sample_call(page_table, split(q).transpose(1, 2, 0, 3), new_rows(k), new_rows(v), bias2,
                            cache_k, cache_v)
        return o.transpose(2, 0, 1, 3).reshape(1, t_new * db, SB_WIDTH).astype(BF16)

    sample_states = (_to_time_major(state_pool[0]), _to_time_major(state_lru_conv[0]), state_lru_h[0][None],
                     _to_time_major(state_ffn_conv[0]), _to_time_major(state_ffn_conv[1]))
    rows_s = db * t_new
    none = jnp.zeros((0, SB_WIDTH), BF16)
    y_s, k_s, v_s, _, _, ss = _trunk(_to_time_major(x_sample.astype(F32)), sample_attend, sample_states, w,
                                     tm=rows_s, in_tile=rows_s, past_kv=(none, none), ts=db, end=rows_s,
                                     pos0=n_pages * PAGE_SIZE)

    heads = lambda a: a.reshape(a.shape[:-1] + (SB_HEADS, SB_HEAD_DIM))
    with_meta = lambda m, r: heads(jnp.concatenate([per_seq(m[:, :N_META]), r], axis=1))[None]
    return (
        y_prompt,
        _from_time_major(y_s, db),
        with_meta(k_meta, k_real),
        with_meta(v_meta, v_real),
        heads(_from_time_major(k_s, db))[None],
        heads(_from_time_major(v_s, db))[None],
        sp[0][None],
        _from_time_major(ss[0], db)[None],
        sp[1][None],
        _from_time_major(ss[1], db)[None],
        sp[2].reshape(1, bp, LRU_WIDTH),
        ss[2].reshape(1, db, LRU_WIDTH),
        jnp.stack([_ffn_ctx_rows(sp[3]), _ffn_ctx_rows(sp[4])]),
        jnp.stack([_from_time_major(_ffn_ctx_rows(ss[3]), db), _from_time_major(_ffn_ctx_rows(ss[4]), db)]),
    )
```

```python
import functools

import jax
import jax.numpy as jnp
from jax import lax
from jax.experimental import pallas as pl
from jax.experimental.pallas import tpu as pltpu

F32 = jnp.float32
BF16 = jnp.bfloat16

D_MODEL = 1024
N_META = 16
SB_HEADS = 8
SB_HEAD_DIM = 64
SB_WIDTH = SB_HEADS * SB_HEAD_DIM
HEAD_PAIRS = SB_HEADS // 2
POOL_WINDOWS = (2, 4, 8, 16)
POOL_WIDTH = 512
POOL_GROUP_DIM = 128
POOL_CTX = 15
LRU_WIDTH = 1024
LRU_BLOCKS = 8
LRU_BLOCK_DIM = 128
LRU_CONV = 4
LRU_C = 8.0
FFN_DIM = 2816
FFN_CONV = 3
FFN_CHUNK = 256
FFN_CHUNKS = FFN_DIM // FFN_CHUNK
LN_EPS = 1e-5
DEPTH = 2
DN_ALPHA = (2.0 * DEPTH) ** 0.25
PAGE_SIZE = 128

LANES = 128
SUBLANES = 8
VMEM_LIMIT = 56 * 1024 * 1024
LOG2E = 1.4426950408889634
NEG_BIG = -1e30


def _round_up(x, m):
    return (x + m - 1) // m * m


def _params(n_axes):
    return pltpu.CompilerParams(dimension_semantics=("arbitrary",) * n_axes,
                                vmem_limit_bytes=VMEM_LIMIT)


def _full_spec(shape):
    nd = len(shape)
    return pl.BlockSpec(shape, lambda *_: (0,) * nd)


def _resident_spec(shape):
    nd = len(shape)
    return pl.BlockSpec(shape, lambda *_: (0,) * nd, pipeline_mode=pl.Buffered(1))


def _layer_norm(y, g, b):
    mu = jnp.mean(y, axis=-1, keepdims=True)
    yc = y - mu
    var = jnp.mean(yc * yc, axis=-1, keepdims=True)
    return yc * lax.rsqrt(var + LN_EPS) * g + b


def _neg_abs(z):
    bits = lax.bitcast_convert_type(z, jnp.uint32) | jnp.uint32(0x80000000)
    return lax.bitcast_convert_type(bits, F32)


def _softplus2(z):
    return jnp.maximum(z, 0.0) + jnp.log(1.0 + jnp.exp2(_neg_abs(z))) * LOG2E


def _dot(a, b):
    return jnp.dot(a, b, preferred_element_type=F32)


def _dot_nt(a, b):
    return lax.dot_general(a, b, (((1,), (1,)), ((), ())), preferred_element_type=F32)


def _neg_upper(n):
    j = lax.broadcasted_iota(jnp.int32, (n, n), 0)
    s = lax.broadcasted_iota(jnp.int32, (n, n), 1)
    return jnp.where(j >= s, -1.0, 0.0).astype(BF16)


def _sb_in_kernel(x_ref, w_ref, kb_init_ref, vb_init_ref, q_ref, k_ref, v_ref, kb_ref, vb_ref, u_ref):
    del kb_init_ref, vb_init_ref
    xb = x_ref[0].astype(BF16)
    q = _dot(xb, w_ref[:, 0:SB_WIDTH])
    q_ref[0] = (q * (LOG2E * SB_HEAD_DIM ** -0.5)).astype(BF16)
    k = _dot(xb, w_ref[:, SB_WIDTH:2 * SB_WIDTH])
    k_ref[0] = k
    kb_ref[0] = k.astype(BF16)
    v = _dot(xb, w_ref[:, 2 * SB_WIDTH:3 * SB_WIDTH])
    v_ref[0] = v
    vb_ref[0] = v.astype(BF16)
    u_ref[0] = _dot(xb, w_ref[:, 3 * SB_WIDTH:3 * SB_WIDTH + POOL_WIDTH])


def _sb_in_call(x, w_in, tm, past_k, past_v):
    nb, rows, _ = x.shape
    past = past_k.shape[0]
    assert past % tm == 0
    behind = lambda p: jnp.concatenate([jnp.broadcast_to(p[None], (nb, past, SB_WIDTH)),
                                        jnp.zeros((nb, rows, SB_WIDTH), BF16)], axis=1)
    tile = lambda c: pl.BlockSpec((1, tm, c), lambda b, i: (b, i, 0))
    behind_past = pl.BlockSpec((1, tm, SB_WIDTH), lambda b, i: (b, i + past // tm, 0))
    any_spec = pl.BlockSpec(memory_space=pl.ANY)
    return pl.pallas_call(
        _sb_in_kernel,
        grid=(nb, rows // tm),
        in_specs=[tile(D_MODEL), _full_spec(w_in.shape), any_spec, any_spec],
        out_specs=[tile(SB_WIDTH), tile(SB_WIDTH), tile(SB_WIDTH), behind_past, behind_past, tile(POOL_WIDTH)],
        out_shape=[jax.ShapeDtypeStruct((nb, rows, SB_WIDTH), BF16),
                   jax.ShapeDtypeStruct((nb, rows, SB_WIDTH), F32),
                   jax.ShapeDtypeStruct((nb, rows, SB_WIDTH), F32),
                   jax.ShapeDtypeStruct((nb, past + rows, SB_WIDTH), BF16),
                   jax.ShapeDtypeStruct((nb, past + rows, SB_WIDTH), BF16),
                   jax.ShapeDtypeStruct((nb, rows, POOL_WIDTH), F32)],
        input_output_aliases={2: 3, 3: 4},
        compiler_params=_params(2),
        name="sb_in",
    )(x, w_in, behind(past_k), behind(past_v))


def _sb_attn_kernel(bias_ref, q_ref, k_ref, v_ref, un_ref, o_ref, acc_ref, car_ref, sp_ref, zc_ref, *,
                    tq, tk, n_past):
    hp = pl.program_id(1)
    qi = pl.program_id(2)
    n_diag = tq // tk
    lane = lax.broadcasted_iota(jnp.int32, (tq, LANES), 1)
    q = q_ref[0]
    zero = jnp.zeros_like(q)
    qh = (jnp.where(lane < SB_HEAD_DIM, q, zero), jnp.where(lane >= SB_HEAD_DIM, q, zero))
    bias = (bias_ref[2 * hp], bias_ref[2 * hp + 1])
    un = un_ref[...]
    row = lax.broadcasted_iota(jnp.int32, (tq, tk), 0)
    col = lax.broadcasted_iota(jnp.int32, (tq, tk), 1)

    acc_ref[...] = jnp.zeros_like(acc_ref)
    car_ref[...] = jnp.zeros_like(car_ref)

    def stage_a(t, par, mask, r0=0):
        kt = k_ref[0, pl.ds(pl.multiple_of(t * tk, tk), tk), :]
        for h in range(2):
            z = _dot_nt(qh[h][r0:], kt) + bias[h]
            if mask is not None:
                z = jnp.where(mask[r0:], z, NEG_BIG)
            sp = _softplus2(z)
            sp_ref[par, h, r0:, :] = sp.astype(BF16)
            car = car_ref[h, r0:, :]
            zc_ref[par, h, r0:, :] = z + jnp.concatenate([car] * (tk // LANES), axis=1)
            car_ref[h, r0:, :] = car - jnp.broadcast_to(jnp.sum(sp, axis=-1, keepdims=True), (tq - r0, LANES))

    def stage_b(t, par, r0=0):
        vt = v_ref[0, pl.ds(pl.multiple_of(t * tk, tk), tk), :]
        for h in range(2):
            incl = _dot(sp_ref[par, h, r0:, :], un)
            a = jnp.exp2(zc_ref[par, h, r0:, :] + incl)
            acc_ref[h, r0:, :] += _dot(a.astype(BF16), vt)

    base = qi * n_diag
    prev = None
    for d in reversed(range(n_diag)):
        t, par = base + 1 + d, (1 + d) % 2
        stage_a(t, par, col + d * tk < row, d * tk)
        if prev is not None:
            stage_b(*prev)
        prev = (t, par, d * tk)

    def body(jj, carry):
        te = base - 2 * jj
        stage_a(te, 0, None)
        stage_b(te + 1, 1)
        stage_a(te - 1, 1, None)
        stage_b(te, 0)
        return carry

    lax.fori_loop(0, base // 2, body, 0)

    stage_a(0, 0, col < n_past)
    stage_b(1, 1)
    stage_b(0, 0)

    o_ref[0] = jnp.where(lane < SB_HEAD_DIM, acc_ref[0], acc_ref[1]).astype(BF16)


def _sb_attn_call(bias, q, k_all, v_all, n_past, tq, tk):
    nb, rows, _ = q.shape
    n_diag = tq // tk
    assert n_diag % 2 == 0 or rows == tq == tk
    tile_spec = pl.BlockSpec((1, tq, LANES), lambda b, h, i: (b, i, h))
    seq_spec = pl.BlockSpec((1, tk + rows, LANES), lambda b, h, i: (b, 0, h))
    return pl.pallas_call(
        functools.partial(_sb_attn_kernel, tq=tq, tk=tk, n_past=n_past),
        grid=(nb, HEAD_PAIRS, rows // tq),
        in_specs=[pl.BlockSpec(memory_space=pltpu.SMEM), tile_spec, seq_spec, seq_spec, _full_spec((tk, tk))],
        out_specs=tile_spec,
        out_shape=jax.ShapeDtypeStruct((nb, rows, SB_WIDTH), BF16),
        scratch_shapes=[pltpu.VMEM((2, tq, LANES), F32), pltpu.VMEM((2, tq, LANES), F32),
                        pltpu.VMEM((2, 2, tq, tk), BF16), pltpu.VMEM((2, 2, tq, tk), F32)],
        compiler_params=_params(3),
        name="sb_attn",
    )(bias, q, k_all, v_all, _neg_upper(tk))


PAGES_PER_STEP = 8


def _sb_sample_kernel(pt_ref, q_ref, kn_ref, vn_ref, bias_ref, un_ref, *refs, n_steps, t_new):
    del pt_ref
    page_refs = refs[:2 * PAGES_PER_STEP]
    o_ref, new_ref, acc_ref, car_ref = refs[2 * PAGES_PER_STEP:]
    j = pl.program_id(1)
    qrows = SB_HEADS * t_new
    bias = bias_ref[...]
    un = un_ref[...]
    qh = [q_ref[0, h].astype(BF16) for h in range(SB_HEADS)]

    def by_head(page):
        return pltpu.einshape("khd->hkd", page).astype(BF16)

    def visit(pages, mask):
        zs = []
        for k_page, _ in pages:
            kh = by_head(k_page)
            z = jnp.concatenate([_dot_nt(qh[h], kh[h]) for h in range(SB_HEADS)], axis=0) + bias
            if mask is not None:
                z = jnp.where(mask, z, NEG_BIG)
            zs.append(z)
        sps = [_softplus2(z) for z in zs]
        incl = _dot(jnp.concatenate(sps, axis=0).astype(BF16), un)
        car = car_ref[...]
        pvs = [None] * SB_HEADS
        for p, (_, v_page) in enumerate(pages):
            a = jnp.exp2(zs[p] + incl[p * qrows:(p + 1) * qrows, :] + car)
            car = car - jnp.broadcast_to(jnp.sum(sps[p], axis=-1, keepdims=True), (qrows, LANES))
            vh = by_head(v_page)
            for h in range(SB_HEADS):
                pv = _dot(a[h * t_new:(h + 1) * t_new, :].astype(BF16), vh[h])
                pvs[h] = pv if pvs[h] is None else pvs[h] + pv
        car_ref[...] = car
        for h in range(SB_HEADS):
            acc_ref[h] += pvs[h]

    @pl.when(j == 0)
    def _():
        acc_ref[...] = jnp.zeros_like(acc_ref)
        car_ref[...] = jnp.zeros_like(car_ref)
        new_ref[...] = jnp.zeros_like(new_ref)
        new_ref[0, 0:t_new] = kn_ref[0]
        new_ref[1, 0:t_new] = vn_ref[0]
        t_of_row = lax.broadcasted_iota(jnp.int32, (qrows, PAGE_SIZE), 0) % t_new
        col = lax.broadcasted_iota(jnp.int32, (qrows, PAGE_SIZE), 1)
        visit([(new_ref[0], new_ref[1])], col < t_of_row)

    visit([(page_refs[p][0, 0], page_refs[PAGES_PER_STEP + p][0, 0]) for p in range(PAGES_PER_STEP)], None)

    @pl.when(j == n_steps - 1)
    def _():
        o_ref[0] = acc_ref[...]


def _sb_sample_call(page_table, q, kn, vn, bias, cache_k, cache_v):
    nb, _, t_new, _ = q.shape
    qrows = SB_HEADS * t_new
    n_pages = page_table.shape[1]
    n_steps = n_pages // PAGES_PER_STEP
    tok_spec = pl.BlockSpec((1, SB_HEADS, t_new, SB_HEAD_DIM), lambda b, j, pt: (b, 0, 0, 0))
    new_spec = pl.BlockSpec((1, t_new, SB_HEADS, SB_HEAD_DIM), lambda b, j, pt: (b, 0, 0, 0))

    def page_spec(p):
        return pl.BlockSpec((1, 1, PAGE_SIZE, SB_HEADS, SB_HEAD_DIM),
                            lambda b, j, pt: (0, pt[b, n_pages - 1 - (j * PAGES_PER_STEP + p)], 0, 0, 0))

    page_specs = [page_spec(p) for p in range(PAGES_PER_STEP)]
    bias_rows = jnp.repeat(bias, t_new)[:, None]
    grid_spec = pltpu.PrefetchScalarGridSpec(
        num_scalar_prefetch=1,
        grid=(nb, n_steps),
        in_specs=[tok_spec, new_spec, new_spec,
                  pl.BlockSpec((qrows, 1), lambda b, j, pt: (0, 0)),
                  pl.BlockSpec((PAGE_SIZE, PAGE_SIZE), lambda b, j, pt: (0, 0))] + page_specs + page_specs,
        out_specs=tok_spec,
        scratch_shapes=[pltpu.VMEM((2, PAGE_SIZE, SB_HEADS, SB_HEAD_DIM), F32),
                        pltpu.VMEM((SB_HEADS, t_new, SB_HEAD_DIM), F32), pltpu.VMEM((qrows, LANES), F32)],
    )
    return pl.pallas_call(
        functools.partial(_sb_sample_kernel, n_steps=n_steps, t_new=t_new),
        grid_spec=grid_spec,
        out_shape=jax.ShapeDtypeStruct((nb, SB_HEADS, t_new, SB_HEAD_DIM), F32),
        compiler_params=_params(2),
        name="sb_sample",
    )(page_table, q, kn, vn, bias_rows, _neg_upper(PAGE_SIZE),
      *([cache_k] * PAGES_PER_STEP), *([cache_v] * PAGES_PER_STEP))


def _pool_layout(ts):
    if ts == 1:
        return 32, (8, 16, 24, 32)
    halo = POOL_CTX * ts
    return halo, (halo - 14 * ts, halo - 12 * ts, halo - 8 * ts, halo)


def _sb_out_kernel(oa_ref, u_ref, pctx_ref, x_ref, wo_ref, pw_ref, ps_ref, g_ref, b_ref,
                   y_ref, nctx_ref, ext_ref, pa_ref, pb_ref, *, tm, ts, n_tiles, end_off, pos0):
    i = pl.program_id(1)
    halo, starts = _pool_layout(ts)
    rows = halo + tm
    ctx_rows = POOL_CTX * ts

    @pl.when(i == 0)
    def _():
        if halo > ctx_rows:
            ext_ref[0:halo - ctx_rows, :] = jnp.zeros((halo - ctx_rows, POOL_WIDTH), F32)
        ext_ref[halo - ctx_rows:halo, :] = pctx_ref[0]

    u = u_ref[0]
    ext_ref[halo:rows, :] = u

    src = ext_ref
    bufs = (pa_ref, pb_ref, pa_ref, pb_ref)
    for s in range(4):
        lo = starts[s]
        shift = (1 << s) * ts
        c0 = s * POOL_GROUP_DIM
        dst = bufs[s]
        dst[lo:rows, c0:] = src[lo:rows, c0:] + src[lo - shift:rows - shift, c0:]
        src = dst

    if pos0 + 1 >= POOL_WINDOWS[-1]:
        inv = [1.0 / w for w in POOL_WINDOWS]
    else:
        pos = pos0 + i * (tm // ts) + lax.broadcasted_iota(jnp.int32, (tm, POOL_GROUP_DIM), 0) // ts
        inv = [1.0 / jnp.minimum(pos + 1, w).astype(F32) for w in POOL_WINDOWS]

    out = _dot(oa_ref[0], wo_ref[0:SB_WIDTH, :])
    for g in range(4):
        c = slice(g * POOL_GROUP_DIM, (g + 1) * POOL_GROUP_DIM)
        d = bufs[g][halo:rows, c] * inv[g] - u[:, c]
        mixed = _dot(d.astype(BF16), pw_ref[g]) * ps_ref[:, c]
        out = out + _dot(mixed.astype(BF16), wo_ref[SB_WIDTH + g * POOL_GROUP_DIM:SB_WIDTH + (g + 1) * POOL_GROUP_DIM, :])

    y_ref[0] = _layer_norm(DN_ALPHA * x_ref[0] + out, g_ref[...], b_ref[...])

    @pl.when(i == n_tiles - 1)
    def _():
        nctx_ref[0] = ext_ref[halo + end_off - ctx_rows:halo + end_off, :]

    if n_tiles > 1:
        ext_ref[halo - 16:halo, :] = ext_ref[rows - 16:rows, :]


def _sb_out_call(oa, u, pctx, x, wo, pw, ps, g, b, *, tm, ts, end, pos0):
    nb, rows, _ = x.shape
    n_tiles = rows // tm
    assert n_tiles == 1 or ts == 1
    assert rows - tm < end <= rows
    halo, _ = _pool_layout(ts)
    ctx_rows = POOL_CTX * ts
    tile = lambda c: pl.BlockSpec((1, tm, c), lambda bi, i: (bi, i, 0))
    ctx_spec = pl.BlockSpec((1, ctx_rows, POOL_WIDTH), lambda bi, i: (bi, 0, 0))
    return pl.pallas_call(
        functools.partial(_sb_out_kernel, tm=tm, ts=ts, n_tiles=n_tiles, end_off=end - (rows - tm), pos0=pos0),
        grid=(nb, n_tiles),
        in_specs=[tile(SB_WIDTH), tile(POOL_WIDTH), ctx_spec, tile(D_MODEL), _full_spec(wo.shape),
                  _full_spec(pw.shape), _full_spec(ps.shape), _full_spec(g.shape), _full_spec(b.shape)],
        out_specs=[tile(D_MODEL), ctx_spec],
        out_shape=[jax.ShapeDtypeStruct((nb, rows, D_MODEL), F32),
                   jax.ShapeDtypeStruct((nb, ctx_rows, POOL_WIDTH), F32)],
        scratch_shapes=[pltpu.VMEM((halo + tm, POOL_WIDTH), F32)] * 3,
        compiler_params=_params(2),
        name="sb_out",
    )(oa, u, pctx, x, wo, pw, ps, g, b)


def _ffn_kernel(x_ref, ctx_ref, wup_ref, cw_ref, cb_ref, wdn_ref, g_ref, b_ref,
                y_ref, nctx_ref, xb_ref, ext_ref, tail_ref, acc_ref, *, tm, ts, n_tiles, end_off):
    i = pl.program_id(1)
    ctx_rows = (FFN_CONV - 1) * ts
    halo = _round_up(ctx_rows, SUBLANES)

    @pl.when(i == 0)
    def _():
        if halo > ctx_rows:
            tail_ref[...] = jnp.zeros_like(tail_ref)
        for c in range(FFN_CHUNKS):
            tail_ref[c, halo - ctx_rows:halo, :] = ctx_ref[0, :, c * FFN_CHUNK:(c + 1) * FFN_CHUNK]

    xb_ref[...] = x_ref[0].astype(BF16)
    acc_ref[...] = jnp.zeros_like(acc_ref)

    def chunk(c, carry):
        xb = xb_ref[...]
        ext_ref[0:halo, :] = tail_ref[c]
        ext_ref[halo:halo + tm, :] = _dot(xb, wup_ref[0, c])
        val = _dot(xb, wup_ref[1, c])
        cw = cw_ref[c]
        gc = cb_ref[c]
        for k in range(FFN_CONV):
            off = halo - (FFN_CONV - 1 - k) * ts
            gc = gc + cw[k:k + 1, :] * ext_ref[off:off + tm, :]
        tail_ref[c] = ext_ref[tm:tm + halo, :]

        @pl.when(i == n_tiles - 1)
        def _():
            nctx_ref[0, c] = ext_ref[halo + end_off - ctx_rows:halo + end_off, :]

        hidden = (jax.nn.gelu(gc) * val).astype(BF16)
        acc_ref[...] += _dot(hidden, wdn_ref[c])
        return carry

    lax.fori_loop(0, FFN_CHUNKS, chunk, 0)
    y_ref[0] = _layer_norm(DN_ALPHA * x_ref[0] + acc_ref[...], g_ref[...], b_ref[...])


def _ffn_call(x, ctx, wup, cw, cb, wdn, g, b, *, tm, ts, end):
    nb, rows, _ = x.shape
    n_tiles = rows // tm
    assert n_tiles == 1 or ts == 1
    assert rows - tm < end <= rows
    ctx_rows = (FFN_CONV - 1) * ts
    halo = _round_up(ctx_rows, SUBLANES)
    tile = pl.BlockSpec((1, tm, D_MODEL), lambda bi, i: (bi, i, 0))
    return pl.pallas_call(
        functools.partial(_ffn_kernel, tm=tm, ts=ts, n_tiles=n_tiles, end_off=end - (rows - tm)),
        grid=(nb, n_tiles),
        in_specs=[tile, pl.BlockSpec((1, ctx_rows, FFN_DIM), lambda bi, i: (bi, 0, 0)),
                  _resident_spec(wup.shape), _full_spec(cw.shape), _full_spec(cb.shape), _resident_spec(wdn.shape),
                  _full_spec(g.shape), _full_spec(b.shape)],
        out_specs=[tile, pl.BlockSpec((1, FFN_CHUNKS, ctx_rows, FFN_CHUNK), lambda bi, i: (bi, 0, 0, 0))],
        out_shape=[jax.ShapeDtypeStruct((nb, rows, D_MODEL), F32),
                   jax.ShapeDtypeStruct((nb, FFN_CHUNKS, ctx_rows, FFN_CHUNK), F32)],
        scratch_shapes=[pltpu.VMEM((tm, D_MODEL), BF16), pltpu.VMEM((halo + tm, FFN_CHUNK), F32),
                        pltpu.VMEM((FFN_CHUNKS, halo, FFN_CHUNK), F32), pltpu.VMEM((tm, D_MODEL), F32)],
        compiler_params=_params(2),
        name="conv_ffn",
    )(x, ctx, wup, cw, cb, wdn, g, b)


def _lru_kernel(x_ref, cctx_ref, h0_ref, win_ref, cw_ref, cb_ref, wa_ref, ba_ref, wx_ref, bx_ref, lam_ref,
                wout_ref, g_ref, b_ref, y_ref, nctx_ref, nh_ref,
                ext_ref, a_ref, hb_ref, gate_ref, h_ref, *, tm, ts, n_tiles, end_off):
    i = pl.program_id(1)
    ctx_rows = (LRU_CONV - 1) * ts
    halo = _round_up(ctx_rows, SUBLANES)

    @pl.when(i == 0)
    def _():
        if halo > ctx_rows:
            ext_ref[0:halo - ctx_rows, :] = jnp.zeros((halo - ctx_rows, LRU_WIDTH), F32)
        ext_ref[halo - ctx_rows:halo, :] = cctx_ref[0]
        h_ref[...] = h0_ref[0]

    xb = x_ref[0].astype(BF16)
    gate_ref[...] = jax.nn.gelu(_dot(xb, win_ref[:, 0:LRU_WIDTH]))
    ext_ref[halo:halo + tm, :] = _dot(xb, win_ref[:, LRU_WIDTH:2 * LRU_WIDTH])

    lam = lam_ref[...]
    neg_c_softplus = -LRU_C * (jnp.maximum(-lam, 0.0) + jnp.log1p(jnp.exp(-jnp.abs(lam))))
    for n in range(LRU_BLOCKS):
        c = slice(n * LRU_BLOCK_DIM, (n + 1) * LRU_BLOCK_DIM)
        xc = cb_ref[:, c]
        for k in range(LRU_CONV):
            off = halo - (LRU_CONV - 1 - k) * ts
            xc = xc + cw_ref[k:k + 1, c] * ext_ref[off:off + tm, c]
        xcb = xc.astype(BF16)
        gate_r = jax.nn.sigmoid(_dot(xcb, wa_ref[n]) + ba_ref[:, c])
        gate_i = jax.nn.sigmoid(_dot(xcb, wx_ref[n]) + bx_ref[:, c])
        log_a = neg_c_softplus[:, c] * gate_r
        a_ref[:, c] = jnp.exp(log_a)
        t = jnp.tanh(-log_a)
        hb_ref[:, c] = jnp.sqrt(2.0 * t / (1.0 + t)) * gate_i * xc

    @pl.when(i == n_tiles - 1)
    def _():
        nctx_ref[0] = ext_ref[halo + end_off - ctx_rows:halo + end_off, :]

    if n_tiles > 1:
        ext_ref[0:halo, :] = ext_ref[tm:tm + halo, :]

    def step(t, h):
        r = pl.ds(pl.multiple_of(t * ts, ts), ts)
        hn = a_ref[r, :] * h + hb_ref[r, :]
        hb_ref[r, :] = hn
        return hn

    h_ref[...] = lax.fori_loop(0, tm // ts, step, h_ref[...], unroll=8 if tm // ts >= 8 else True)

    @pl.when(i == n_tiles - 1)
    def _():
        nh_ref[0] = hb_ref[end_off - ts:end_off, :]

    mixed = (gate_ref[...] * hb_ref[...]).astype(BF16)
    y_ref[0] = _layer_norm(DN_ALPHA * x_ref[0] + _dot(mixed, wout_ref[...]), g_ref[...], b_ref[...])


def _lru_call(x, cctx, h0, win, cw, cb, wa, ba, wx, bx, lam, wout, g, b, *, tm, ts, end):
    nb, rows, _ = x.shape
    n_tiles = rows // tm
    assert n_tiles == 1 or ts == 1
    assert rows - tm < end <= rows
    ctx_rows = (LRU_CONV - 1) * ts
    halo = _round_up(ctx_rows, SUBLANES)
    tile = pl.BlockSpec((1, tm, D_MODEL), lambda bi, i: (bi, i, 0))
    ctx_spec = pl.BlockSpec((1, ctx_rows, LRU_WIDTH), lambda bi, i: (bi, 0, 0))
    h_spec = pl.BlockSpec((1, ts, LRU_WIDTH), lambda bi, i: (bi, 0, 0))
    consts = [win, cw, cb, wa, ba, wx, bx, lam, wout, g, b]
    return pl.pallas_call(
        functools.partial(_lru_kernel, tm=tm, ts=ts, n_tiles=n_tiles, end_off=end - (rows - tm)),
        grid=(nb, n_tiles),
        in_specs=[tile, ctx_spec, h_spec] + [_full_spec(c.shape) for c in consts],
        out_specs=[tile, ctx_spec, h_spec],
        out_shape=[jax.ShapeDtypeStruct((nb, rows, D_MODEL), F32),
                   jax.ShapeDtypeStruct((nb, ctx_rows, LRU_WIDTH), F32),
                   jax.ShapeDtypeStruct((nb, ts, LRU_WIDTH), F32)],
        scratch_shapes=[pltpu.VMEM((halo + tm, LRU_WIDTH), F32), pltpu.VMEM((tm, LRU_WIDTH), F32),
                        pltpu.VMEM((tm, LRU_WIDTH), F32), pltpu.VMEM((tm, LRU_WIDTH), F32),
                        pltpu.VMEM((ts, LRU_WIDTH), F32)],
        compiler_params=_params(2),
        name="rg_lru",
    )(x, cctx, h0, *consts)


def _trunk(x, attend, states, w, *, tm, in_tile, ffn_tile, past_kv, ts, end, pos0):
    pool_ctx, lru_cctx, lru_h0, ffn_ctx0, ffn_ctx1 = states
    q, k, v, kb, vb, u = _sb_in_call(x, w["sb_w_in"], in_tile, *past_kv)
    oa = attend(q, k, v, kb, vb)
    x, new_pool = _sb_out_call(oa, u, pool_ctx, x, w["sb_w_out"], w["pool_w"], w["pool_scale"],
                               w["ln_g"][0], w["ln_b"][0], tm=tm, ts=ts, end=end, pos0=pos0)
    x, new_ffn0 = _ffn_call(x, ffn_ctx0, w["ffn_up"][0], w["ffn_cw"][0], w["ffn_cb"][0], w["ffn_dn"][0],
                            w["ln_g"][1], w["ln_b"][1], tm=ffn_tile, ts=ts, end=end)
    x, new_cctx, new_h = _lru_call(x, lru_cctx, lru_h0, w["lru_w_in"], w["lru_conv_w"], w["lru_conv_b"],
                                   w["lru_w_a"], w["lru_b_a"], w["lru_w_x"], w["lru_b_x"], w["lru_lambda"],
                                   w["lru_w_out"], w["ln_g"][2], w["ln_b"][2], tm=tm, ts=ts, end=end)
    x, new_ffn1 = _ffn_call(x, ffn_ctx1, w["ffn_up"][1], w["ffn_cw"][1], w["ffn_cb"][1], w["ffn_dn"][1],
                            w["ln_g"][3], w["ln_b"][3], tm=ffn_tile, ts=ts, end=end)
    return x, k, v, kb, vb, (new_pool, new_cctx, new_h, new_ffn0, new_ffn1)


def _ffn_ctx_rows(c):
    nb, _, r, _ = c.shape
    return c.transpose(0, 2, 1, 3).reshape(nb, r, FFN_DIM)


def _to_time_major(s):
    db, t, c = s.shape
    return s.transpose(1, 0, 2).reshape(1, t * db, c)


def _from_time_major(s, db):
    _, r, c = s.shape
    return s.reshape(r // db, db, c).transpose(1, 0, 2)


META_TILE = 128
PROMPT_TILE = 512
FFN_TILE = 1024
ATTN_TQ = 512
ATTN_TK = 256


def kernel(x_prompt, x_sample, cache_sb_k, cache_sb_v, page_table, state_pool, state_lru_conv, state_lru_h,
           state_ffn_conv, meta_tokens, sb_w_in, sb_logit_bias, sb_w_out, pool_w, pool_scale, lru_w_in,
           lru_conv_w, lru_conv_b, lru_w_a, lru_b_a, lru_w_x, lru_b_x, lru_lambda, lru_w_out, ffn_w_up,
           ffn_conv_w, ffn_conv_b, ffn_w_down, ln_g, ln_b):
    bp, seq, _ = x_prompt.shape
    db, t_new, _ = x_sample.shape
    row = lambda a: a.reshape(1, -1).astype(F32)
    w = dict(
        sb_w_in=sb_w_in[0].astype(BF16), sb_w_out=sb_w_out[0].astype(BF16), pool_w=pool_w[0].astype(BF16),
        pool_scale=row(pool_scale[0]),
        lru_w_in=lru_w_in[0].astype(BF16), lru_conv_w=lru_conv_w[0].astype(F32), lru_conv_b=row(lru_conv_b[0]),
        lru_w_a=lru_w_a[0].astype(BF16), lru_b_a=row(lru_b_a[0]), lru_w_x=lru_w_x[0].astype(BF16),
        lru_b_x=row(lru_b_x[0]), lru_lambda=row(lru_lambda[0]), lru_w_out=lru_w_out[0].astype(BF16),
        ffn_up=[ffn_w_up[l].reshape(D_MODEL, 2, FFN_CHUNKS, FFN_CHUNK).transpose(1, 2, 0, 3).astype(BF16)
                for l in range(DEPTH)],
        ffn_dn=[ffn_w_down[l].reshape(FFN_CHUNKS, FFN_CHUNK, D_MODEL).astype(BF16) for l in range(DEPTH)],
        ffn_cw=[ffn_conv_w[l].reshape(FFN_CONV, FFN_CHUNKS, FFN_CHUNK).transpose(1, 0, 2).astype(F32)
                for l in range(DEPTH)],
        ffn_cb=[ffn_conv_b[l].reshape(FFN_CHUNKS, 1, FFN_CHUNK).astype(F32) for l in range(DEPTH)],
        ln_g=[row(ln_g[l, s]) for l in range(DEPTH) for s in range(2)],
        ln_b=[row(ln_b[l, s]) for l in range(DEPTH) for s in range(2)],
    )
    bias2 = sb_logit_bias[0].astype(F32) * LOG2E

    x_meta = jnp.concatenate([meta_tokens.astype(F32), jnp.zeros((META_TILE - N_META, D_MODEL), F32)])[None]
    zero_states = (jnp.zeros((1, POOL_CTX, POOL_WIDTH), F32), jnp.zeros((1, LRU_CONV - 1, LRU_WIDTH), F32),
                   jnp.zeros((1, 1, LRU_WIDTH), F32), jnp.zeros((1, FFN_CONV - 1, FFN_DIM), F32),
                   jnp.zeros((1, FFN_CONV - 1, FFN_DIM), F32))
    no_past = jnp.zeros((META_TILE, SB_WIDTH), BF16)
    meta_attend = lambda q, k, v, kb, vb: _sb_attn_call(bias2, q, kb, vb, 0, META_TILE, META_TILE)
    _, k_meta, v_meta, kb_meta, vb_meta, st = _trunk(x_meta, meta_attend, zero_states, w, tm=META_TILE,
                                                      in_tile=META_TILE, ffn_tile=META_TILE, past_kv=(no_past, no_past), ts=1,
                                                      end=N_META, pos0=0)

    per_seq = lambda s: jnp.broadcast_to(s, (bp,) + s.shape[1:])
    prompt_states = (per_seq(st[0]), per_seq(st[1]), per_seq(st[2]),
                     per_seq(_ffn_ctx_rows(st[3])), per_seq(_ffn_ctx_rows(st[4])))
    meta_block = lambda a: jnp.concatenate([a[0, META_TILE:], jnp.zeros((ATTN_TK - META_TILE, SB_WIDTH), BF16)])
    prompt_attend = lambda q, k, v, kb, vb: _sb_attn_call(bias2, q, kb, vb, N_META, ATTN_TQ, ATTN_TK)
    y_prompt, k_real, v_real, _, _, sp = _trunk(x_prompt.astype(F32), prompt_attend, prompt_states, w,
                                                tm=PROMPT_TILE, in_tile=ATTN_TK, ffn_tile=FFN_TILE,
                                                past_kv=(meta_block(kb_meta), meta_block(vb_meta)), ts=1, end=seq,
                                                pos0=N_META)

    n_pages = page_table.shape[1]
    cache_k, cache_v = cache_sb_k, cache_sb_v

    def sample_attend(q, k, v, kb, vb):
        split = lambda a: a.astype(F32).reshape(t_new, db, SB_HEADS, SB_HEAD_DIM)
        new_rows = lambda a: split(a).transpose(1, 0, 2, 3)
        o = _sb_sample_call(page_table, split(q).transpose(1, 2, 0, 3), new_rows(k), new_rows(v), bias2,
                            cache_k, cache_v)
        return o.transpose(2, 0, 1, 3).reshape(1, t_new * db, SB_WIDTH).astype(BF16)

    sample_states = (_to_time_major(state_pool[0]), _to_time_major(state_lru_conv[0]), state_lru_h[0][None],
                     _to_time_major(state_ffn_conv[0]), _to_time_major(state_ffn_conv[1]))
    rows_s = db * t_new
    none = jnp.zeros((0, SB_WIDTH), BF16)
    y_s, k_s, v_s, _, _, ss = _trunk(_to_time_major(x_sample.astype(F32)), sample_attend, sample_states, w,
                                     tm=rows_s, in_tile=rows_s, ffn_tile=rows_s, past_kv=(none, none), ts=db, end=rows_s,
                                     pos0=n_pages * PAGE_SIZE)

    heads = lambda a: a.reshape(a.shape[:-1] + (SB_HEADS, SB_HEAD_DIM))
    with_meta = lambda m, r: heads(jnp.concatenate([per_seq(m[:, :N_META]), r], axis=1))[None]
    return (
        y_prompt,
        _from_time_major(y_s, db),
        with_meta(k_meta, k_real),
        with_meta(v_meta, v_real),
        heads(_from_time_major(k_s, db))[None],
        heads(_from_time_major(v_s, db))[None],
        sp[0][None],
        _from_time_major(ss[0], db)[None],
        sp[1][None],
        _from_time_major(ss[1], db)[None],
        sp[2].reshape(1, bp, LRU_WIDTH),
        ss[2].reshape(1, db, LRU_WIDTH),
        jnp.stack([_ffn_ctx_rows(sp[3]), _ffn_ctx_rows(sp[4])]),
        jnp.stack([_from_time_major(_ffn_ctx_rows(ss[3]), db), _from_time_major(_ffn_ctx_rows(ss[4]), db)]),
    )
```

```python
import functools

import jax
import jax.numpy as jnp
from jax import lax
from jax.experimental import pallas as pl
from jax.experimental.pallas import tpu as pltpu

F32 = jnp.float32
BF16 = jnp.bfloat16

D_MODEL = 1024
N_META = 16
SB_HEADS = 8
SB_HEAD_DIM = 64
SB_WIDTH = SB_HEADS * SB_HEAD_DIM
HEAD_PAIRS = SB_HEADS // 2
POOL_WINDOWS = (2, 4, 8, 16)
POOL_WIDTH = 512
POOL_GROUP_DIM = 128
POOL_CTX = 15
LRU_WIDTH = 1024
LRU_BLOCKS = 8
LRU_BLOCK_DIM = 128
LRU_CONV = 4
LRU_C = 8.0
FFN_DIM = 2816
FFN_CONV = 3
FFN_CHUNK = 256
FFN_CHUNKS = FFN_DIM // FFN_CHUNK
LN_EPS = 1e-5
DEPTH = 2
DN_ALPHA = (2.0 * DEPTH) ** 0.25
PAGE_SIZE = 128

LANES = 128
SUBLANES = 8
VMEM_LIMIT = 56 * 1024 * 1024
LOG2E = 1.4426950408889634
NEG_BIG = -1e30


def _round_up(x, m):
    return (x + m - 1) // m * m


def _params(n_axes):
    return pltpu.CompilerParams(dimension_semantics=("arbitrary",) * n_axes,
                                vmem_limit_bytes=VMEM_LIMIT)


def _full_spec(shape):
    nd = len(shape)
    return pl.BlockSpec(shape, lambda *_: (0,) * nd)


def _resident_spec(shape):
    nd = len(shape)
    return pl.BlockSpec(shape, lambda *_: (0,) * nd, pipeline_mode=pl.Buffered(1))


def _layer_norm(y, g, b):
    mu = jnp.mean(y, axis=-1, keepdims=True)
    yc = y - mu
    var = jnp.mean(yc * yc, axis=-1, keepdims=True)
    return yc * lax.rsqrt(var + LN_EPS) * g + b


def _neg_abs(z):
    bits = lax.bitcast_convert_type(z, jnp.uint32) | jnp.uint32(0x80000000)
    return lax.bitcast_convert_type(bits, F32)


def _softplus2(z):
    return jnp.maximum(z, 0.0) + jnp.log(1.0 + jnp.exp2(_neg_abs(z))) * LOG2E


def _dot(a, b):
    return jnp.dot(a, b, preferred_element_type=F32)


def _dot_nt(a, b):
    return lax.dot_general(a, b, (((1,), (1,)), ((), ())), preferred_element_type=F32)


def _neg_upper(n):
    j = lax.broadcasted_iota(jnp.int32, (n, n), 0)
    s = lax.broadcasted_iota(jnp.int32, (n, n), 1)
    return jnp.where(j >= s, -1.0, 0.0).astype(BF16)


def _sb_in_kernel(x_ref, w_ref, kb_init_ref, vb_init_ref, q_ref, k_ref, v_ref, kb_ref, vb_ref, u_ref):
    del kb_init_ref, vb_init_ref
    xb = x_ref[0].astype(BF16)
    q = _dot(xb, w_ref[:, 0:SB_WIDTH])
    q_ref[0] = (q * (LOG2E * SB_HEAD_DIM ** -0.5)).astype(BF16)
    k = _dot(xb, w_ref[:, SB_WIDTH:2 * SB_WIDTH])
    k_ref[0] = k
    kb_ref[0] = k.astype(BF16)
    v = _dot(xb, w_ref[:, 2 * SB_WIDTH:3 * SB_WIDTH])
    v_ref[0] = v
    vb_ref[0] = v.astype(BF16)
    u_ref[0] = _dot(xb, w_ref[:, 3 * SB_WIDTH:3 * SB_WIDTH + POOL_WIDTH])


def _sb_in_call(x, w_in, tm, past_k, past_v):
    nb, rows, _ = x.shape
    past = past_k.shape[0]
    assert past % tm == 0
    behind = lambda p: jnp.concatenate([jnp.broadcast_to(p[None], (nb, past, SB_WIDTH)),
                                        jnp.zeros((nb, rows, SB_WIDTH), BF16)], axis=1)
    tile = lambda c: pl.BlockSpec((1, tm, c), lambda b, i: (b, i, 0))
    behind_past = pl.BlockSpec((1, tm, SB_WIDTH), lambda b, i: (b, i + past // tm, 0))
    any_spec = pl.BlockSpec(memory_space=pl.ANY)
    return pl.pallas_call(
        _sb_in_kernel,
        grid=(nb, rows // tm),
        in_specs=[tile(D_MODEL), _full_spec(w_in.shape), any_spec, any_spec],
        out_specs=[tile(SB_WIDTH), tile(SB_WIDTH), tile(SB_WIDTH), behind_past, behind_past, tile(POOL_WIDTH)],
        out_shape=[jax.ShapeDtypeStruct((nb, rows, SB_WIDTH), BF16),
                   jax.ShapeDtypeStruct((nb, rows, SB_WIDTH), F32),
                   jax.ShapeDtypeStruct((nb, rows, SB_WIDTH), F32),
                   jax.ShapeDtypeStruct((nb, past + rows, SB_WIDTH), BF16),
                   jax.ShapeDtypeStruct((nb, past + rows, SB_WIDTH), BF16),
                   jax.ShapeDtypeStruct((nb, rows, POOL_WIDTH), F32)],
        input_output_aliases={2: 3, 3: 4},
        compiler_params=_params(2),
        name="sb_in",
    )(x, w_in, behind(past_k), behind(past_v))


def _sb_attn_kernel(bias_ref, q_ref, k_ref, v_ref, un_ref, o_ref, acc_ref, car_ref, sp_ref, zc_ref, *,
                    tq, tk, n_past):
    hp = pl.program_id(1)
    qi = pl.program_id(2)
    n_diag = tq // tk
    lane = lax.broadcasted_iota(jnp.int32, (tq, LANES), 1)
    q = q_ref[0]
    zero = jnp.zeros_like(q)
    qh = (jnp.where(lane < SB_HEAD_DIM, q, zero), jnp.where(lane >= SB_HEAD_DIM, q, zero))
    bias = (bias_ref[2 * hp], bias_ref[2 * hp + 1])
    un = un_ref[...]
    row = lax.broadcasted_iota(jnp.int32, (tq, tk), 0)
    col = lax.broadcasted_iota(jnp.int32, (tq, tk), 1)

    acc_ref[...] = jnp.zeros_like(acc_ref)
    car_ref[...] = jnp.zeros_like(car_ref)

    def stage_a(t, par, mask, r0=0):
        kt = k_ref[0, pl.ds(pl.multiple_of(t * tk, tk), tk), :]
        for h in range(2):
            z = _dot_nt(qh[h][r0:], kt) + bias[h]
            if mask is not None:
                z = jnp.where(mask[r0:], z, NEG_BIG)
            sp = _softplus2(z)
            sp_ref[par, h, r0:, :] = sp.astype(BF16)
            car = car_ref[h, r0:, :]
            zc_ref[par, h, r0:, :] = z + jnp.concatenate([car] * (tk // LANES), axis=1)
            car_ref[h, r0:, :] = car - jnp.broadcast_to(jnp.sum(sp, axis=-1, keepdims=True), (tq - r0, LANES))

    def stage_b(t, par, r0=0):
        vt = v_ref[0, pl.ds(pl.multiple_of(t * tk, tk), tk), :]
        for h in range(2):
            incl = _dot(sp_ref[par, h, r0:, :], un)
            a = jnp.exp2(zc_ref[par, h, r0:, :] + incl)
            acc_ref[h, r0:, :] += _dot(a.astype(BF16), vt)

    base = qi * n_diag
    prev = None
    for d in reversed(range(n_diag)):
        t, par = base + 1 + d, (1 + d) % 2
        stage_a(t, par, col + d * tk < row, d * tk)
        if prev is not None:
            stage_b(*prev)
        prev = (t, par, d * tk)

    def body(jj, carry):
        te = base - 2 * jj
        stage_a(te, 0, None)
        stage_b(te + 1, 1)
        stage_a(te - 1, 1, None)
        stage_b(te, 0)
        return carry

    lax.fori_loop(0, base // 2, body, 0)

    stage_a(0, 0, col < n_past)
    stage_b(1, 1)
    stage_b(0, 0)

    o_ref[0] = jnp.where(lane < SB_HEAD_DIM, acc_ref[0], acc_ref[1]).astype(BF16)


def _sb_attn_call(bias, q, k_all, v_all, n_past, tq, tk):
    nb, rows, _ = q.shape
    n_diag = tq // tk
    assert n_diag % 2 == 0 or rows == tq == tk
    tile_spec = pl.BlockSpec((1, tq, LANES), lambda b, h, i: (b, i, h))
    seq_spec = pl.BlockSpec((1, tk + rows, LANES), lambda b, h, i: (b, 0, h))
    return pl.pallas_call(
        functools.partial(_sb_attn_kernel, tq=tq, tk=tk, n_past=n_past),
        grid=(nb, HEAD_PAIRS, rows // tq),
        in_specs=[pl.BlockSpec(memory_space=pltpu.SMEM), tile_spec, seq_spec, seq_spec, _full_spec((tk, tk))],
        out_specs=tile_spec,
        out_shape=jax.ShapeDtypeStruct((nb, rows, SB_WIDTH), BF16),
        scratch_shapes=[pltpu.VMEM((2, tq, LANES), F32), pltpu.VMEM((2, tq, LANES), F32),
                        pltpu.VMEM((2, 2, tq, tk), BF16), pltpu.VMEM((2, 2, tq, tk), F32)],
        compiler_params=_params(3),
        name="sb_attn",
    )(bias, q, k_all, v_all, _neg_upper(tk))


PAGES_PER_STEP = 8


def _sb_sample_kernel(pt_ref, q_ref, kn_ref, vn_ref, bias_ref, un_ref, *refs, n_steps, t_new):
    del pt_ref
    page_refs = refs[:2 * PAGES_PER_STEP]
    o_ref, acc_ref, car_ref = refs[2 * PAGES_PER_STEP:]
    j = pl.program_id(1)
    qrows = SB_HEADS * t_new
    bias = bias_ref[...]
    un = un_ref[...]
    qh = [q_ref[0, h].astype(BF16) for h in range(SB_HEADS)]

    def visit(pages, mask):
        zs = []
        for k_page, _ in pages:
            kt = k_page.astype(BF16)
            z = jnp.concatenate([_dot(qh[h], kt[h]) for h in range(SB_HEADS)], axis=0) + bias
            if mask is not None:
                z = jnp.where(mask, z, NEG_BIG)
            zs.append(z)
        sps = [_softplus2(z) for z in zs]
        incl = _dot(jnp.concatenate(sps, axis=0).astype(BF16), un)
        car = car_ref[...]
        pvs = [None] * SB_HEADS
        for p, (_, v_page) in enumerate(pages):
            a = jnp.exp2(zs[p] + incl[p * qrows:(p + 1) * qrows, :] + car)
            car = car - jnp.broadcast_to(jnp.sum(sps[p], axis=-1, keepdims=True), (qrows, LANES))
            vt = v_page.astype(BF16)
            for h in range(SB_HEADS):
                pv = _dot_nt(a[h * t_new:(h + 1) * t_new, :].astype(BF16), vt[h])
                pvs[h] = pv if pvs[h] is None else pvs[h] + pv
        car_ref[...] = car
        for h in range(SB_HEADS):
            acc_ref[h] += pvs[h]

    @pl.when(j == 0)
    def _():
        acc_ref[...] = jnp.zeros_like(acc_ref)
        car_ref[...] = jnp.zeros_like(car_ref)
        t_of_row = lax.broadcasted_iota(jnp.int32, (qrows, PAGE_SIZE), 0) % t_new
        col = lax.broadcasted_iota(jnp.int32, (qrows, PAGE_SIZE), 1)
        visit([(kn_ref[0], vn_ref[0])], col < t_of_row)

    visit([(page_refs[p][0, 0], page_refs[PAGES_PER_STEP + p][0, 0]) for p in range(PAGES_PER_STEP)], None)

    @pl.when(j == n_steps - 1)
    def _():
        o_ref[0] = acc_ref[...]


def _sb_sample_call(page_table, q, kn, vn, bias, cache_kt, cache_vt):
    nb, _, t_new, _ = q.shape
    qrows = SB_HEADS * t_new
    n_pages = page_table.shape[1]
    n_steps = n_pages // PAGES_PER_STEP
    tok_spec = pl.BlockSpec((1, SB_HEADS, t_new, SB_HEAD_DIM), lambda b, j, pt: (b, 0, 0, 0))
    new_spec = pl.BlockSpec((1, SB_HEADS, SB_HEAD_DIM, PAGE_SIZE), lambda b, j, pt: (b, 0, 0, 0))

    def page_spec(p):
        return pl.BlockSpec((1, 1, SB_HEADS, SB_HEAD_DIM, PAGE_SIZE),
                            lambda b, j, pt: (0, pt[b, n_pages - 1 - (j * PAGES_PER_STEP + p)], 0, 0, 0))

    page_specs = [page_spec(p) for p in range(PAGES_PER_STEP)]
    bias_rows = jnp.repeat(bias, t_new)[:, None]
    grid_spec = pltpu.PrefetchScalarGridSpec(
        num_scalar_prefetch=1,
        grid=(nb, n_steps),
        in_specs=[tok_spec, new_spec, new_spec,
                  pl.BlockSpec((qrows, 1), lambda b, j, pt: (0, 0)),
                  pl.BlockSpec((PAGE_SIZE, PAGE_SIZE), lambda b, j, pt: (0, 0))] + page_specs + page_specs,
        out_specs=tok_spec,
        scratch_shapes=[pltpu.VMEM((SB_HEADS, t_new, SB_HEAD_DIM), F32), pltpu.VMEM((qrows, LANES), F32)],
    )
    return pl.pallas_call(
        functools.partial(_sb_sample_kernel, n_steps=n_steps, t_new=t_new),
        grid_spec=grid_spec,
        out_shape=jax.ShapeDtypeStruct((nb, SB_HEADS, t_new, SB_HEAD_DIM), F32),
        compiler_params=_params(2),
        name="sb_sample",
    )(page_table, q, kn, vn, bias_rows, _neg_upper(PAGE_SIZE),
      *([cache_kt] * PAGES_PER_STEP), *([cache_vt] * PAGES_PER_STEP))


def _pool_layout(ts):
    if ts == 1:
        return 32, (8, 16, 24, 32)
    halo = POOL_CTX * ts
    return halo, (halo - 14 * ts, halo - 12 * ts, halo - 8 * ts, halo)


def _sb_out_kernel(oa_ref, u_ref, pctx_ref, x_ref, wo_ref, pw_ref, ps_ref, g_ref, b_ref,
                   y_ref, nctx_ref, ext_ref, pa_ref, pb_ref, *, tm, ts, n_tiles, end_off, pos0):
    i = pl.program_id(1)
    halo, starts = _pool_layout(ts)
    rows = halo + tm
    ctx_rows = POOL_CTX * ts

    @pl.when(i == 0)
    def _():
        if halo > ctx_rows:
            ext_ref[0:halo - ctx_rows, :] = jnp.zeros((halo - ctx_rows, POOL_WIDTH), F32)
        ext_ref[halo - ctx_rows:halo, :] = pctx_ref[0]

    u = u_ref[0]
    ext_ref[halo:rows, :] = u

    src = ext_ref
    bufs = (pa_ref, pb_ref, pa_ref, pb_ref)
    for s in range(4):
        lo = starts[s]
        shift = (1 << s) * ts
        c0 = s * POOL_GROUP_DIM
        dst = bufs[s]
        dst[lo:rows, c0:] = src[lo:rows, c0:] + src[lo - shift:rows - shift, c0:]
        src = dst

    if pos0 + 1 >= POOL_WINDOWS[-1]:
        inv = [1.0 / w for w in POOL_WINDOWS]
    else:
        pos = pos0 + i * (tm // ts) + lax.broadcasted_iota(jnp.int32, (tm, POOL_GROUP_DIM), 0) // ts
        inv = [1.0 / jnp.minimum(pos + 1, w).astype(F32) for w in POOL_WINDOWS]

    out = _dot(oa_ref[0], wo_ref[0:SB_WIDTH, :])
    for g in range(4):
        c = slice(g * POOL_GROUP_DIM, (g + 1) * POOL_GROUP_DIM)
        d = bufs[g][halo:rows, c] * inv[g] - u[:, c]
        mixed = _dot(d.astype(BF16), pw_ref[g]) * ps_ref[:, c]
        out = out + _dot(mixed.astype(BF16), wo_ref[SB_WIDTH + g * POOL_GROUP_DIM:SB_WIDTH + (g + 1) * POOL_GROUP_DIM, :])

    y_ref[0] = _layer_norm(DN_ALPHA * x_ref[0] + out, g_ref[...], b_ref[...])

    @pl.when(i == n_tiles - 1)
    def _():
        nctx_ref[0] = ext_ref[halo + end_off - ctx_rows:halo + end_off, :]

    if n_tiles > 1:
        ext_ref[halo - 16:halo, :] = ext_ref[rows - 16:rows, :]


def _sb_out_call(oa, u, pctx, x, wo, pw, ps, g, b, *, tm, ts, end, pos0):
    nb, rows, _ = x.shape
    n_tiles = rows // tm
    assert n_tiles == 1 or ts == 1
    assert rows - tm < end <= rows
    halo, _ = _pool_layout(ts)
    ctx_rows = POOL_CTX * ts
    tile = lambda c: pl.BlockSpec((1, tm, c), lambda bi, i: (bi, i, 0))
    ctx_spec = pl.BlockSpec((1, ctx_rows, POOL_WIDTH), lambda bi, i: (bi, 0, 0))
    return pl.pallas_call(
        functools.partial(_sb_out_kernel, tm=tm, ts=ts, n_tiles=n_tiles, end_off=end - (rows - tm), pos0=pos0),
        grid=(nb, n_tiles),
        in_specs=[tile(SB_WIDTH), tile(POOL_WIDTH), ctx_spec, tile(D_MODEL), _full_spec(wo.shape),
                  _full_spec(pw.shape), _full_spec(ps.shape), _full_spec(g.shape), _full_spec(b.shape)],
        out_specs=[tile(D_MODEL), ctx_spec],
        out_shape=[jax.ShapeDtypeStruct((nb, rows, D_MODEL), F32),
                   jax.ShapeDtypeStruct((nb, ctx_rows, POOL_WIDTH), F32)],
        scratch_shapes=[pltpu.VMEM((halo + tm, POOL_WIDTH), F32)] * 3,
        compiler_params=_params(2),
        name="sb_out",
    )(oa, u, pctx, x, wo, pw, ps, g, b)


def _ffn_kernel(x_ref, ctx_ref, wup_ref, cw_ref, cb_ref, wdn_ref, g_ref, b_ref,
                y_ref, nctx_ref, xb_ref, ext_ref, tail_ref, acc_ref, *, tm, ts, n_tiles, end_off):
    i = pl.program_id(1)
    ctx_rows = (FFN_CONV - 1) * ts
    halo = _round_up(ctx_rows, SUBLANES)

    @pl.when(i == 0)
    def _():
        if halo > ctx_rows:
            tail_ref[...] = jnp.zeros_like(tail_ref)
        for c in range(FFN_CHUNKS):
            tail_ref[c, halo - ctx_rows:halo, :] = ctx_ref[0, :, c * FFN_CHUNK:(c + 1) * FFN_CHUNK]

    xb_ref[...] = x_ref[0].astype(BF16)
    acc_ref[...] = jnp.zeros_like(acc_ref)

    def chunk(c, carry):
        xb = xb_ref[...]
        ext_ref[0:halo, :] = tail_ref[c]
        ext_ref[halo:halo + tm, :] = _dot(xb, wup_ref[0, c])
        val = _dot(xb, wup_ref[1, c])
        cw = cw_ref[c]
        gc = cb_ref[c]
        for k in range(FFN_CONV):
            off = halo - (FFN_CONV - 1 - k) * ts
            gc = gc + cw[k:k + 1, :] * ext_ref[off:off + tm, :]
        tail_ref[c] = ext_ref[tm:tm + halo, :]

        @pl.when(i == n_tiles - 1)
        def _():
            nctx_ref[0, c] = ext_ref[halo + end_off - ctx_rows:halo + end_off, :]

        hidden = (jax.nn.gelu(gc) * val).astype(BF16)
        acc_ref[...] += _dot(hidden, wdn_ref[c])
        return carry

    lax.fori_loop(0, FFN_CHUNKS, chunk, 0)
    y_ref[0] = _layer_norm(DN_ALPHA * x_ref[0] + acc_ref[...], g_ref[...], b_ref[...])


def _ffn_call(x, ctx, wup, cw, cb, wdn, g, b, *, tm, ts, end):
    nb, rows, _ = x.shape
    n_tiles = rows // tm
    assert n_tiles == 1 or ts == 1
    assert rows - tm < end <= rows
    ctx_rows = (FFN_CONV - 1) * ts
    halo = _round_up(ctx_rows, SUBLANES)
    tile = pl.BlockSpec((1, tm, D_MODEL), lambda bi, i: (bi, i, 0))
    return pl.pallas_call(
        functools.partial(_ffn_kernel, tm=tm, ts=ts, n_tiles=n_tiles, end_off=end - (rows - tm)),
        grid=(nb, n_tiles),
        in_specs=[tile, pl.BlockSpec((1, ctx_rows, FFN_DIM), lambda bi, i: (bi, 0, 0)),
                  _resident_spec(wup.shape), _full_spec(cw.shape), _full_spec(cb.shape), _resident_spec(wdn.shape),
                  _full_spec(g.shape), _full_spec(b.shape)],
        out_specs=[tile, pl.BlockSpec((1, FFN_CHUNKS, ctx_rows, FFN_CHUNK), lambda bi, i: (bi, 0, 0, 0))],
        out_shape=[jax.ShapeDtypeStruct((nb, rows, D_MODEL), F32),
                   jax.ShapeDtypeStruct((nb, FFN_CHUNKS, ctx_rows, FFN_CHUNK), F32)],
        scratch_shapes=[pltpu.VMEM((tm, D_MODEL), BF16), pltpu.VMEM((halo + tm, FFN_CHUNK), F32),
                        pltpu.VMEM((FFN_CHUNKS, halo, FFN_CHUNK), F32), pltpu.VMEM((tm, D_MODEL), F32)],
        compiler_params=_params(2),
        name="conv_ffn",
    )(x, ctx, wup, cw, cb, wdn, g, b)


def _lru_kernel(x_ref, cctx_ref, h0_ref, win_ref, cw_ref, cb_ref, wa_ref, ba_ref, wx_ref, bx_ref, lam_ref,
                wout_ref, g_ref, b_ref, y_ref, nctx_ref, nh_ref,
                ext_ref, a_ref, hb_ref, gate_ref, h_ref, *, tm, ts, n_tiles, end_off):
    i = pl.program_id(1)
    ctx_rows = (LRU_CONV - 1) * ts
    halo = _round_up(ctx_rows, SUBLANES)

    @pl.when(i == 0)
    def _():
        if halo > ctx_rows:
            ext_ref[0:halo - ctx_rows, :] = jnp.zeros((halo - ctx_rows, LRU_WIDTH), F32)
        ext_ref[halo - ctx_rows:halo, :] = cctx_ref[0]
        h_ref[...] = h0_ref[0]

    xb = x_ref[0].astype(BF16)
    gate_ref[...] = jax.nn.gelu(_dot(xb, win_ref[:, 0:LRU_WIDTH]))
    ext_ref[halo:halo + tm, :] = _dot(xb, win_ref[:, LRU_WIDTH:2 * LRU_WIDTH])

    lam = lam_ref[...]
    neg_c_softplus = -LRU_C * (jnp.maximum(-lam, 0.0) + jnp.log1p(jnp.exp(-jnp.abs(lam))))
    for n in range(LRU_BLOCKS):
        c = slice(n * LRU_BLOCK_DIM, (n + 1) * LRU_BLOCK_DIM)
        xc = cb_ref[:, c]
        for k in range(LRU_CONV):
            off = halo - (LRU_CONV - 1 - k) * ts
            xc = xc + cw_ref[k:k + 1, c] * ext_ref[off:off + tm, c]
        xcb = xc.astype(BF16)
        gate_r = jax.nn.sigmoid(_dot(xcb, wa_ref[n]) + ba_ref[:, c])
        gate_i = jax.nn.sigmoid(_dot(xcb, wx_ref[n]) + bx_ref[:, c])
        log_a = neg_c_softplus[:, c] * gate_r
        a_ref[:, c] = jnp.exp(log_a)
        t = jnp.tanh(-log_a)
        hb_ref[:, c] = jnp.sqrt(2.0 * t / (1.0 + t)) * gate_i * xc

    @pl.when(i == n_tiles - 1)
    def _():
        nctx_ref[0] = ext_ref[halo + end_off - ctx_rows:halo + end_off, :]

    if n_tiles > 1:
        ext_ref[0:halo, :] = ext_ref[tm:tm + halo, :]

    def step(t, h):
        r = pl.ds(pl.multiple_of(t * ts, ts), ts)
        hn = a_ref[r, :] * h + hb_ref[r, :]
        hb_ref[r, :] = hn
        return hn

    h_ref[...] = lax.fori_loop(0, tm // ts, step, h_ref[...], unroll=8 if tm // ts >= 8 else True)

    @pl.when(i == n_tiles - 1)
    def _():
        nh_ref[0] = hb_ref[end_off - ts:end_off, :]

    mixed = (gate_ref[...] * hb_ref[...]).astype(BF16)
    y_ref[0] = _layer_norm(DN_ALPHA * x_ref[0] + _dot(mixed, wout_ref[...]), g_ref[...], b_ref[...])


def _lru_call(x, cctx, h0, win, cw, cb, wa, ba, wx, bx, lam, wout, g, b, *, tm, ts, end):
    nb, rows, _ = x.shape
    n_tiles = rows // tm
    assert n_tiles == 1 or ts == 1
    assert rows - tm < end <= rows
    ctx_rows = (LRU_CONV - 1) * ts
    halo = _round_up(ctx_rows, SUBLANES)
    tile = pl.BlockSpec((1, tm, D_MODEL), lambda bi, i: (bi, i, 0))
    ctx_spec = pl.BlockSpec((1, ctx_rows, LRU_WIDTH), lambda bi, i: (bi, 0, 0))
    h_spec = pl.BlockSpec((1, ts, LRU_WIDTH), lambda bi, i: (bi, 0, 0))
    consts = [win, cw, cb, wa, ba, wx, bx, lam, wout, g, b]
    return pl.pallas_call(
        functools.partial(_lru_kernel, tm=tm, ts=ts, n_tiles=n_tiles, end_off=end - (rows - tm)),
        grid=(nb, n_tiles),
        in_specs=[tile, ctx_spec, h_spec] + [_full_spec(c.shape) for c in consts],
        out_specs=[tile, ctx_spec, h_spec],
        out_shape=[jax.ShapeDtypeStruct((nb, rows, D_MODEL), F32),
                   jax.ShapeDtypeStruct((nb, ctx_rows, LRU_WIDTH), F32),
                   jax.ShapeDtypeStruct((nb, ts, LRU_WIDTH), F32)],
        scratch_shapes=[pltpu.VMEM((halo + tm, LRU_WIDTH), F32), pltpu.VMEM((tm, LRU_WIDTH), F32),
                        pltpu.VMEM((tm, LRU_WIDTH), F32), pltpu.VMEM((tm, LRU_WIDTH), F32),
                        pltpu.VMEM((ts, LRU_WIDTH), F32)],
        compiler_params=_params(2),
        name="rg_lru",
    )(x, cctx, h0, *consts)


def _trunk(x, attend, states, w, *, tm, in_tile, ffn_tile, past_kv, ts, end, pos0):
    pool_ctx, lru_cctx, lru_h0, ffn_ctx0, ffn_ctx1 = states
    q, k, v, kb, vb, u = _sb_in_call(x, w["sb_w_in"], in_tile, *past_kv)
    oa = attend(q, k, v, kb, vb)
    x, new_pool = _sb_out_call(oa, u, pool_ctx, x, w["sb_w_out"], w["pool_w"], w["pool_scale"],
                               w["ln_g"][0], w["ln_b"][0], tm=tm, ts=ts, end=end, pos0=pos0)
    x, new_ffn0 = _ffn_call(x, ffn_ctx0, w["ffn_up"][0], w["ffn_cw"][0], w["ffn_cb"][0], w["ffn_dn"][0],
                            w["ln_g"][1], w["ln_b"][1], tm=ffn_tile, ts=ts, end=end)
    x, new_cctx, new_h = _lru_call(x, lru_cctx, lru_h0, w["lru_w_in"], w["lru_conv_w"], w["lru_conv_b"],
                                   w["lru_w_a"], w["lru_b_a"], w["lru_w_x"], w["lru_b_x"], w["lru_lambda"],
                                   w["lru_w_out"], w["ln_g"][2], w["ln_b"][2], tm=tm, ts=ts, end=end)
    x, new_ffn1 = _ffn_call(x, ffn_ctx1, w["ffn_up"][1], w["ffn_cw"][1], w["ffn_cb"][1], w["ffn_dn"][1],
                            w["ln_g"][3], w["ln_b"][3], tm=ffn_tile, ts=ts, end=end)
    return x, k, v, kb, vb, (new_pool, new_cctx, new_h, new_ffn0, new_ffn1)


def _ffn_ctx_rows(c):
    nb, _, r, _ = c.shape
    return c.transpose(0, 2, 1, 3).reshape(nb, r, FFN_DIM)


def _to_time_major(s):
    db, t, c = s.shape
    return s.transpose(1, 0, 2).reshape(1, t * db, c)


def _from_time_major(s, db):
    _, r, c = s.shape
    return s.reshape(r // db, db, c).transpose(1, 0, 2)


META_TILE = 128
PROMPT_TILE = 512
FFN_TILE = 1024
ATTN_TQ = 512
ATTN_TK = 256


def kernel(x_prompt, x_sample, cache_sb_k, cache_sb_v, page_table, state_pool, state_lru_conv, state_lru_h,
           state_ffn_conv, meta_tokens, sb_w_in, sb_logit_bias, sb_w_out, pool_w, pool_scale, lru_w_in,
           lru_conv_w, lru_conv_b, lru_w_a, lru_b_a, lru_w_x, lru_b_x, lru_lambda, lru_w_out, ffn_w_up,
           ffn_conv_w, ffn_conv_b, ffn_w_down, ln_g, ln_b):
    bp, seq, _ = x_prompt.shape
    db, t_new, _ = x_sample.shape
    row = lambda a: a.reshape(1, -1).astype(F32)
    w = dict(
        sb_w_in=sb_w_in[0].astype(BF16), sb_w_out=sb_w_out[0].astype(BF16), pool_w=pool_w[0].astype(BF16),
        pool_scale=row(pool_scale[0]),
        lru_w_in=lru_w_in[0].astype(BF16), lru_conv_w=lru_conv_w[0].astype(F32), lru_conv_b=row(lru_conv_b[0]),
        lru_w_a=lru_w_a[0].astype(BF16), lru_b_a=row(lru_b_a[0]), lru_w_x=lru_w_x[0].astype(BF16),
        lru_b_x=row(lru_b_x[0]), lru_lambda=row(lru_lambda[0]), lru_w_out=lru_w_out[0].astype(BF16),
        ffn_up=[ffn_w_up[l].reshape(D_MODEL, 2, FFN_CHUNKS, FFN_CHUNK).transpose(1, 2, 0, 3).astype(BF16)
                for l in range(DEPTH)],
        ffn_dn=[ffn_w_down[l].reshape(FFN_CHUNKS, FFN_CHUNK, D_MODEL).astype(BF16) for l in range(DEPTH)],
        ffn_cw=[ffn_conv_w[l].reshape(FFN_CONV, FFN_CHUNKS, FFN_CHUNK).transpose(1, 0, 2).astype(F32)
                for l in range(DEPTH)],
        ffn_cb=[ffn_conv_b[l].reshape(FFN_CHUNKS, 1, FFN_CHUNK).astype(F32) for l in range(DEPTH)],
        ln_g=[row(ln_g[l, s]) for l in range(DEPTH) for s in range(2)],
        ln_b=[row(ln_b[l, s]) for l in range(DEPTH) for s in range(2)],
    )
    bias2 = sb_logit_bias[0].astype(F32) * LOG2E

    x_meta = jnp.concatenate([meta_tokens.astype(F32), jnp.zeros((META_TILE - N_META, D_MODEL), F32)])[None]
    zero_states = (jnp.zeros((1, POOL_CTX, POOL_WIDTH), F32), jnp.zeros((1, LRU_CONV - 1, LRU_WIDTH), F32),
                   jnp.zeros((1, 1, LRU_WIDTH), F32), jnp.zeros((1, FFN_CONV - 1, FFN_DIM), F32),
                   jnp.zeros((1, FFN_CONV - 1, FFN_DIM), F32))
    no_past = jnp.zeros((META_TILE, SB_WIDTH), BF16)
    meta_attend = lambda q, k, v, kb, vb: _sb_attn_call(bias2, q, kb, vb, 0, META_TILE, META_TILE)
    _, k_meta, v_meta, kb_meta, vb_meta, st = _trunk(x_meta, meta_attend, zero_states, w, tm=META_TILE,
                                                      in_tile=META_TILE, ffn_tile=META_TILE, past_kv=(no_past, no_past), ts=1,
                                                      end=N_META, pos0=0)

    per_seq = lambda s: jnp.broadcast_to(s, (bp,) + s.shape[1:])
    prompt_states = (per_seq(st[0]), per_seq(st[1]), per_seq(st[2]),
                     per_seq(_ffn_ctx_rows(st[3])), per_seq(_ffn_ctx_rows(st[4])))
    meta_block = lambda a: jnp.concatenate([a[0, META_TILE:], jnp.zeros((ATTN_TK - META_TILE, SB_WIDTH), BF16)])
    prompt_attend = lambda q, k, v, kb, vb: _sb_attn_call(bias2, q, kb, vb, N_META, ATTN_TQ, ATTN_TK)
    y_prompt, k_real, v_real, _, _, sp = _trunk(x_prompt.astype(F32), prompt_attend, prompt_states, w,
                                                tm=PROMPT_TILE, in_tile=ATTN_TK, ffn_tile=FFN_TILE,
                                                past_kv=(meta_block(kb_meta), meta_block(vb_meta)), ts=1, end=seq,
                                                pos0=N_META)

    n_pages = page_table.shape[1]
    cache_kt = jnp.transpose(cache_sb_k, (0, 1, 3, 4, 2))
    cache_vt = jnp.transpose(cache_sb_v, (0, 1, 3, 4, 2))

    def sample_attend(q, k, v, kb, vb):
        split = lambda a: a.astype(F32).reshape(t_new, db, SB_HEADS, SB_HEAD_DIM)
        new_page = lambda a: jnp.pad(split(a).transpose(1, 2, 3, 0),
                                     ((0, 0), (0, 0), (0, 0), (0, PAGE_SIZE - t_new)))
        o = _sb_sample_call(page_table, split(q).transpose(1, 2, 0, 3), new_page(k), new_page(v), bias2,
                            cache_kt, cache_vt)
        return o.transpose(2, 0, 1, 3).reshape(1, t_new * db, SB_WIDTH).astype(BF16)

    sample_states = (_to_time_major(state_pool[0]), _to_time_major(state_lru_conv[0]), state_lru_h[0][None],
                     _to_time_major(state_ffn_conv[0]), _to_time_major(state_ffn_conv[1]))
    rows_s = db * t_new
    none = jnp.zeros((0, SB_WIDTH), BF16)
    y_s, k_s, v_s, _, _, ss = _trunk(_to_time_major(x_sample.astype(F32)), sample_attend, sample_states, w,
                                     tm=rows_s, in_tile=rows_s, ffn_tile=rows_s, past_kv=(none, none), ts=db, end=rows_s,
                                     pos0=n_pages * PAGE_SIZE)

    heads = lambda a: a.reshape(a.shape[:-1] + (SB_HEADS, SB_HEAD_DIM))
    with_meta = lambda m, r: heads(jnp.concatenate([per_seq(m[:, :N_META]), r], axis=1))[None]
    return (
        y_prompt,
        _from_time_major(y_s, db),
        with_meta(k_meta, k_real),
        with_meta(v_meta, v_real),
        heads(_from_time_major(k_s, db))[None],
        heads(_from_time_major(v_s, db))[None],
        sp[0][None],
        _from_time_major(ss[0], db)[None],
        sp[1][None],
        _from_time_major(ss[1], db)[None],
        sp[2].reshape(1, bp, LRU_WIDTH),
        ss[2].reshape(1, db, LRU_WIDTH),
        jnp.stack([_ffn_ctx_rows(sp[3]), _ffn_ctx_rows(sp[4])]),
        jnp.stack([_from_time_major(_ffn_ctx_rows(ss[3]), db), _from_time_major(_ffn_ctx_rows(ss[4]), db)]),
    )
```

```python
import functools

import jax
import jax.numpy as jnp
from jax import lax
from jax.experimental import pallas as pl
from jax.experimental.pallas import tpu as pltpu

F32 = jnp.float32
BF16 = jnp.bfloat16

D_MODEL = 1024
N_META = 16
SB_HEADS = 8
SB_HEAD_DIM = 64
SB_WIDTH = SB_HEADS * SB_HEAD_DIM
HEAD_PAIRS = SB_HEADS // 2
POOL_WINDOWS = (2, 4, 8, 16)
POOL_WIDTH = 512
POOL_GROUP_DIM = 128
POOL_CTX = 15
LRU_WIDTH = 1024
LRU_BLOCKS = 8
LRU_BLOCK_DIM = 128
LRU_CONV = 4
LRU_C = 8.0
FFN_DIM = 2816
FFN_CONV = 3
FFN_CHUNK = 256
FFN_CHUNKS = FFN_DIM // FFN_CHUNK
LN_EPS = 1e-5
DEPTH = 2
DN_ALPHA = (2.0 * DEPTH) ** 0.25
PAGE_SIZE = 128

LANES = 128
SUBLANES = 8
VMEM_LIMIT = 56 * 1024 * 1024
LOG2E = 1.4426950408889634
NEG_BIG = -1e30


def _round_up(x, m):
    return (x + m - 1) // m * m


def _params(n_axes):
    return pltpu.CompilerParams(dimension_semantics=("arbitrary",) * n_axes,
                                vmem_limit_bytes=VMEM_LIMIT)


def _full_spec(shape):
    nd = len(shape)
    return pl.BlockSpec(shape, lambda *_: (0,) * nd)


def _resident_spec(shape):
    nd = len(shape)
    return pl.BlockSpec(shape, lambda *_: (0,) * nd, pipeline_mode=pl.Buffered(1))


def _layer_norm(y, g, b):
    mu = jnp.mean(y, axis=-1, keepdims=True)
    yc = y - mu
    var = jnp.mean(yc * yc, axis=-1, keepdims=True)
    return yc * lax.rsqrt(var + LN_EPS) * g + b


def _neg_abs(z):
    bits = lax.bitcast_convert_type(z, jnp.uint32) | jnp.uint32(0x80000000)
    return lax.bitcast_convert_type(bits, F32)


def _softplus2(z):
    return jnp.maximum(z, 0.0) + jnp.log(1.0 + jnp.exp2(_neg_abs(z))) * LOG2E


def _dot(a, b):
    return jnp.dot(a, b, preferred_element_type=F32)


def _dot_nt(a, b):
    return lax.dot_general(a, b, (((1,), (1,)), ((), ())), preferred_element_type=F32)


def _neg_upper(n):
    j = lax.broadcasted_iota(jnp.int32, (n, n), 0)
    s = lax.broadcasted_iota(jnp.int32, (n, n), 1)
    return jnp.where(j >= s, -1.0, 0.0).astype(BF16)


def _sb_in_kernel(x_ref, w_ref, kb_init_ref, vb_init_ref, q_ref, k_ref, v_ref, kb_ref, vb_ref, u_ref):
    del kb_init_ref, vb_init_ref
    xb = x_ref[0].astype(BF16)
    q = _dot(xb, w_ref[:, 0:SB_WIDTH])
    q_ref[0] = (q * (LOG2E * SB_HEAD_DIM ** -0.5)).astype(BF16)
    k = _dot(xb, w_ref[:, SB_WIDTH:2 * SB_WIDTH])
    k_ref[0] = k
    kb_ref[0] = k.astype(BF16)
    v = _dot(xb, w_ref[:, 2 * SB_WIDTH:3 * SB_WIDTH])
    v_ref[0] = v
    vb_ref[0] = v.astype(BF16)
    u_ref[0] = _dot(xb, w_ref[:, 3 * SB_WIDTH:3 * SB_WIDTH + POOL_WIDTH])


def _sb_in_call(x, w_in, tm, past_k, past_v):
    nb, rows, _ = x.shape
    past = past_k.shape[0]
    assert past % tm == 0
    behind = lambda p: jnp.concatenate([jnp.broadcast_to(p[None], (nb, past, SB_WIDTH)),
                                        jnp.zeros((nb, rows, SB_WIDTH), BF16)], axis=1)
    tile = lambda c: pl.BlockSpec((1, tm, c), lambda b, i: (b, i, 0))
    behind_past = pl.BlockSpec((1, tm, SB_WIDTH), lambda b, i: (b, i + past // tm, 0))
    any_spec = pl.BlockSpec(memory_space=pl.ANY)
    return pl.pallas_call(
        _sb_in_kernel,
        grid=(nb, rows // tm),
        in_specs=[tile(D_MODEL), _full_spec(w_in.shape), any_spec, any_spec],
        out_specs=[tile(SB_WIDTH), tile(SB_WIDTH), tile(SB_WIDTH), behind_past, behind_past, tile(POOL_WIDTH)],
        out_shape=[jax.ShapeDtypeStruct((nb, rows, SB_WIDTH), BF16),
                   jax.ShapeDtypeStruct((nb, rows, SB_WIDTH), F32),
                   jax.ShapeDtypeStruct((nb, rows, SB_WIDTH), F32),
                   jax.ShapeDtypeStruct((nb, past + rows, SB_WIDTH), BF16),
                   jax.ShapeDtypeStruct((nb, past + rows, SB_WIDTH), BF16),
                   jax.ShapeDtypeStruct((nb, rows, POOL_WIDTH), F32)],
        input_output_aliases={2: 3, 3: 4},
        compiler_params=_params(2),
        name="sb_in",
    )(x, w_in, behind(past_k), behind(past_v))


def _sb_attn_kernel(bias_ref, q_ref, k_ref, v_ref, un_ref, o_ref, acc_ref, car_ref, sp_ref, zc_ref, *,
                    tq, tk, n_past):
    hp = pl.program_id(1)
    qi = pl.program_id(2)
    n_diag = tq // tk
    lane = lax.broadcasted_iota(jnp.int32, (tq, LANES), 1)
    q = q_ref[0]
    zero = jnp.zeros_like(q)
    qh = (jnp.where(lane < SB_HEAD_DIM, q, zero), jnp.where(lane >= SB_HEAD_DIM, q, zero))
    bias = (bias_ref[2 * hp], bias_ref[2 * hp + 1])
    un = un_ref[...]
    row = lax.broadcasted_iota(jnp.int32, (tq, tk), 0)
    col = lax.broadcasted_iota(jnp.int32, (tq, tk), 1)

    acc_ref[...] = jnp.zeros_like(acc_ref)
    car_ref[...] = jnp.zeros_like(car_ref)

    def stage_a(t, par, mask, r0=0):
        kt = k_ref[0, pl.ds(pl.multiple_of(t * tk, tk), tk), :]
        for h in range(2):
            z = _dot_nt(qh[h][r0:], kt) + bias[h]
            if mask is not None:
                z = jnp.where(mask[r0:], z, NEG_BIG)
            sp = _softplus2(z)
            sp_ref[par, h, r0:, :] = sp.astype(BF16)
            car = car_ref[h, r0:, :]
            zc_ref[par, h, r0:, :] = z + jnp.concatenate([car] * (tk // LANES), axis=1)
            car_ref[h, r0:, :] = car - jnp.broadcast_to(jnp.sum(sp, axis=-1, keepdims=True), (tq - r0, LANES))

    def stage_b(t, par, r0=0):
        vt = v_ref[0, pl.ds(pl.multiple_of(t * tk, tk), tk), :]
        for h in range(2):
            incl = _dot(sp_ref[par, h, r0:, :], un)
            a = jnp.exp2(zc_ref[par, h, r0:, :] + incl)
            acc_ref[h, r0:, :] += _dot(a.astype(BF16), vt)

    base = qi * n_diag
    prev = None
    for d in reversed(range(n_diag)):
        t, par = base + 1 + d, (1 + d) % 2
        stage_a(t, par, col + d * tk < row, d * tk)
        if prev is not None:
            stage_b(*prev)
        prev = (t, par, d * tk)

    def body(jj, carry):
        te = base - 2 * jj
        stage_a(te, 0, None)
        stage_b(te + 1, 1)
        stage_a(te - 1, 1, None)
        stage_b(te, 0)
        return carry

    lax.fori_loop(0, base // 2, body, 0)

    stage_a(0, 0, col < n_past)
    stage_b(1, 1)
    stage_b(0, 0)

    o_ref[0] = jnp.where(lane < SB_HEAD_DIM, acc_ref[0], acc_ref[1]).astype(BF16)


def _sb_attn_call(bias, q, k_all, v_all, n_past, tq, tk):
    nb, rows, _ = q.shape
    n_diag = tq // tk
    assert n_diag % 2 == 0 or rows == tq == tk
    tile_spec = pl.BlockSpec((1, tq, LANES), lambda b, h, i: (b, i, h))
    seq_spec = pl.BlockSpec((1, tk + rows, LANES), lambda b, h, i: (b, 0, h))
    return pl.pallas_call(
        functools.partial(_sb_attn_kernel, tq=tq, tk=tk, n_past=n_past),
        grid=(nb, HEAD_PAIRS, rows // tq),
        in_specs=[pl.BlockSpec(memory_space=pltpu.SMEM), tile_spec, seq_spec, seq_spec, _full_spec((tk, tk))],
        out_specs=tile_spec,
        out_shape=jax.ShapeDtypeStruct((nb, rows, SB_WIDTH), BF16),
        scratch_shapes=[pltpu.VMEM((2, tq, LANES), F32), pltpu.VMEM((2, tq, LANES), F32),
                        pltpu.VMEM((2, 2, tq, tk), BF16), pltpu.VMEM((2, 2, tq, tk), F32)],
        compiler_params=_params(3),
        name="sb_attn",
    )(bias, q, k_all, v_all, _neg_upper(tk))


PAGES_PER_STEP = 16


def _sb_sample_kernel(pt_ref, q_ref, kn_ref, vn_ref, bias_ref, un_ref, *refs, n_steps, t_new):
    del pt_ref
    page_refs = refs[:2 * PAGES_PER_STEP]
    o_ref, acc_ref, car_ref = refs[2 * PAGES_PER_STEP:]
    j = pl.program_id(1)
    qrows = SB_HEADS * t_new
    bias = bias_ref[...]
    un = un_ref[...]
    qh = [q_ref[0, h].astype(BF16) for h in range(SB_HEADS)]

    def visit(pages, mask):
        zs = []
        for k_page, _ in pages:
            kt = k_page.astype(BF16)
            z = jnp.concatenate([_dot(qh[h], kt[h]) for h in range(SB_HEADS)], axis=0) + bias
            if mask is not None:
                z = jnp.where(mask, z, NEG_BIG)
            zs.append(z)
        sps = [_softplus2(z) for z in zs]
        incl = _dot(jnp.concatenate(sps, axis=0).astype(BF16), un)
        car = car_ref[...]
        pvs = [None] * SB_HEADS
        for p, (_, v_page) in enumerate(pages):
            a = jnp.exp2(zs[p] + incl[p * qrows:(p + 1) * qrows, :] + car)
            car = car - jnp.broadcast_to(jnp.sum(sps[p], axis=-1, keepdims=True), (qrows, LANES))
            vt = v_page.astype(BF16)
            for h in range(SB_HEADS):
                pv = _dot_nt(a[h * t_new:(h + 1) * t_new, :].astype(BF16), vt[h])
                pvs[h] = pv if pvs[h] is None else pvs[h] + pv
        car_ref[...] = car
        for h in range(SB_HEADS):
            acc_ref[h] += pvs[h]

    @pl.when(j == 0)
    def _():
        acc_ref[...] = jnp.zeros_like(acc_ref)
        car_ref[...] = jnp.zeros_like(car_ref)
        t_of_row = lax.broadcasted_iota(jnp.int32, (qrows, PAGE_SIZE), 0) % t_new
        col = lax.broadcasted_iota(jnp.int32, (qrows, PAGE_SIZE), 1)
        visit([(kn_ref[0], vn_ref[0])], col < t_of_row)

    visit([(page_refs[p][0, 0], page_refs[PAGES_PER_STEP + p][0, 0]) for p in range(PAGES_PER_STEP)], None)

    @pl.when(j == n_steps - 1)
    def _():
        o_ref[0] = acc_ref[...]


def _sb_sample_call(page_table, q, kn, vn, bias, cache_kt, cache_vt):
    nb, _, t_new, _ = q.shape
    qrows = SB_HEADS * t_new
    n_pages = page_table.shape[1]
    n_steps = n_pages // PAGES_PER_STEP
    tok_spec = pl.BlockSpec((1, SB_HEADS, t_new, SB_HEAD_DIM), lambda b, j, pt: (b, 0, 0, 0))
    new_spec = pl.BlockSpec((1, SB_HEADS, SB_HEAD_DIM, PAGE_SIZE), lambda b, j, pt: (b, 0, 0, 0))

    def page_spec(p):
        return pl.BlockSpec((1, 1, SB_HEADS, SB_HEAD_DIM, PAGE_SIZE),
                            lambda b, j, pt: (0, pt[b, n_pages - 1 - (j * PAGES_PER_STEP + p)], 0, 0, 0))

    page_specs = [page_spec(p) for p in range(PAGES_PER_STEP)]
    bias_rows = jnp.repeat(bias, t_new)[:, None]
    grid_spec = pltpu.PrefetchScalarGridSpec(
        num_scalar_prefetch=1,
        grid=(nb, n_steps),
        in_specs=[tok_spec, new_spec, new_spec,
                  pl.BlockSpec((qrows, 1), lambda b, j, pt: (0, 0)),
                  pl.BlockSpec((PAGE_SIZE, PAGE_SIZE), lambda b, j, pt: (0, 0))] + page_specs + page_specs,
        out_specs=tok_spec,
        scratch_shapes=[pltpu.VMEM((SB_HEADS, t_new, SB_HEAD_DIM), F32), pltpu.VMEM((qrows, LANES), F32)],
    )
    return pl.pallas_call(
        functools.partial(_sb_sample_kernel, n_steps=n_steps, t_new=t_new),
        grid_spec=grid_spec,
        out_shape=jax.ShapeDtypeStruct((nb, SB_HEADS, t_new, SB_HEAD_DIM), F32),
        compiler_params=_params(2),
        name="sb_sample",
    )(page_table, q, kn, vn, bias_rows, _neg_upper(PAGE_SIZE),
      *([cache_kt] * PAGES_PER_STEP), *([cache_vt] * PAGES_PER_STEP))


def _pool_layout(ts):
    if ts == 1:
        return 32, (8, 16, 24, 32)
    halo = POOL_CTX * ts
    return halo, (halo - 14 * ts, halo - 12 * ts, halo - 8 * ts, halo)


def _sb_out_kernel(oa_ref, u_ref, pctx_ref, x_ref, wo_ref, pw_ref, ps_ref, g_ref, b_ref,
                   y_ref, nctx_ref, ext_ref, pa_ref, pb_ref, *, tm, ts, n_tiles, end_off, pos0):
    i = pl.program_id(1)
    halo, starts = _pool_layout(ts)
    rows = halo + tm
    ctx_rows = POOL_CTX * ts

    @pl.when(i == 0)
    def _():
        if halo > ctx_rows:
            ext_ref[0:halo - ctx_rows, :] = jnp.zeros((halo - ctx_rows, POOL_WIDTH), F32)
        ext_ref[halo - ctx_rows:halo, :] = pctx_ref[0]

    u = u_ref[0]
    ext_ref[halo:rows, :] = u

    src = ext_ref
    bufs = (pa_ref, pb_ref, pa_ref, pb_ref)
    for s in range(4):
        lo = starts[s]
        shift = (1 << s) * ts
        c0 = s * POOL_GROUP_DIM
        dst = bufs[s]
        dst[lo:rows, c0:] = src[lo:rows, c0:] + src[lo - shift:rows - shift, c0:]
        src = dst

    if pos0 + 1 >= POOL_WINDOWS[-1]:
        inv = [1.0 / w for w in POOL_WINDOWS]
    else:
        pos = pos0 + i * (tm // ts) + lax.broadcasted_iota(jnp.int32, (tm, POOL_GROUP_DIM), 0) // ts
        inv = [1.0 / jnp.minimum(pos + 1, w).astype(F32) for w in POOL_WINDOWS]

    out = _dot(oa_ref[0], wo_ref[0:SB_WIDTH, :])
    for g in range(4):
        c = slice(g * POOL_GROUP_DIM, (g + 1) * POOL_GROUP_DIM)
        d = bufs[g][halo:rows, c] * inv[g] - u[:, c]
        mixed = _dot(d.astype(BF16), pw_ref[g]) * ps_ref[:, c]
        out = out + _dot(mixed.astype(BF16), wo_ref[SB_WIDTH + g * POOL_GROUP_DIM:SB_WIDTH + (g + 1) * POOL_GROUP_DIM, :])

    y_ref[0] = _layer_norm(DN_ALPHA * x_ref[0] + out, g_ref[...], b_ref[...])

    @pl.when(i == n_tiles - 1)
    def _():
        nctx_ref[0] = ext_ref[halo + end_off - ctx_rows:halo + end_off, :]

    if n_tiles > 1:
        ext_ref[halo - 16:halo, :] = ext_ref[rows - 16:rows, :]


def _sb_out_call(oa, u, pctx, x, wo, pw, ps, g, b, *, tm, ts, end, pos0):
    nb, rows, _ = x.shape
    n_tiles = rows // tm
    assert n_tiles == 1 or ts == 1
    assert rows - tm < end <= rows
    halo, _ = _pool_layout(ts)
    ctx_rows = POOL_CTX * ts
    tile = lambda c: pl.BlockSpec((1, tm, c), lambda bi, i: (bi, i, 0))
    ctx_spec = pl.BlockSpec((1, ctx_rows, POOL_WIDTH), lambda bi, i: (bi, 0, 0))
    return pl.pallas_call(
        functools.partial(_sb_out_kernel, tm=tm, ts=ts, n_tiles=n_tiles, end_off=end - (rows - tm), pos0=pos0),
        grid=(nb, n_tiles),
        in_specs=[tile(SB_WIDTH), tile(POOL_WIDTH), ctx_spec, tile(D_MODEL), _full_spec(wo.shape),
                  _full_spec(pw.shape), _full_spec(ps.shape), _full_spec(g.shape), _full_spec(b.shape)],
        out_specs=[tile(D_MODEL), ctx_spec],
        out_shape=[jax.ShapeDtypeStruct((nb, rows, D_MODEL), F32),
                   jax.ShapeDtypeStruct((nb, ctx_rows, POOL_WIDTH), F32)],
        scratch_shapes=[pltpu.VMEM((halo + tm, POOL_WIDTH), F32)] * 3,
        compiler_params=_params(2),
        name="sb_out",
    )(oa, u, pctx, x, wo, pw, ps, g, b)


def _ffn_kernel(x_ref, ctx_ref, wup_ref, cw_ref, cb_ref, wdn_ref, g_ref, b_ref,
                y_ref, nctx_ref, xb_ref, ext_ref, tail_ref, acc_ref, *, tm, ts, n_tiles, end_off):
    i = pl.program_id(1)
    ctx_rows = (FFN_CONV - 1) * ts
    halo = _round_up(ctx_rows, SUBLANES)

    @pl.when(i == 0)
    def _():
        if halo > ctx_rows:
            tail_ref[...] = jnp.zeros_like(tail_ref)
        for c in range(FFN_CHUNKS):
            tail_ref[c, halo - ctx_rows:halo, :] = ctx_ref[0, :, c * FFN_CHUNK:(c + 1) * FFN_CHUNK]

    xb_ref[...] = x_ref[0].astype(BF16)
    acc_ref[...] = jnp.zeros_like(acc_ref)

    def chunk(c, carry):
        xb = xb_ref[...]
        ext_ref[0:halo, :] = tail_ref[c]
        ext_ref[halo:halo + tm, :] = _dot(xb, wup_ref[0, c])
        val = _dot(xb, wup_ref[1, c])
        cw = cw_ref[c]
        gc = cb_ref[c]
        for k in range(FFN_CONV):
            off = halo - (FFN_CONV - 1 - k) * ts
            gc = gc + cw[k:k + 1, :] * ext_ref[off:off + tm, :]
        tail_ref[c] = ext_ref[tm:tm + halo, :]

        @pl.when(i == n_tiles - 1)
        def _():
            nctx_ref[0, c] = ext_ref[halo + end_off - ctx_rows:halo + end_off, :]

        hidden = (jax.nn.gelu(gc) * val).astype(BF16)
        acc_ref[...] += _dot(hidden, wdn_ref[c])
        return carry

    lax.fori_loop(0, FFN_CHUNKS, chunk, 0)
    y_ref[0] = _layer_norm(DN_ALPHA * x_ref[0] + acc_ref[...], g_ref[...], b_ref[...])


def _ffn_call(x, ctx, wup, cw, cb, wdn, g, b, *, tm, ts, end):
    nb, rows, _ = x.shape
    n_tiles = rows // tm
    assert n_tiles == 1 or ts == 1
    assert rows - tm < end <= rows
    ctx_rows = (FFN_CONV - 1) * ts
    halo = _round_up(ctx_rows, SUBLANES)
    tile = pl.BlockSpec((1, tm, D_MODEL), lambda bi, i: (bi, i, 0))
    return pl.pallas_call(
        functools.partial(_ffn_kernel, tm=tm, ts=ts, n_tiles=n_tiles, end_off=end - (rows - tm)),
        grid=(nb, n_tiles),
        in_specs=[tile, pl.BlockSpec((1, ctx_rows, FFN_DIM), lambda bi, i: (bi, 0, 0)),
                  _resident_spec(wup.shape), _full_spec(cw.shape), _full_spec(cb.shape), _resident_spec(wdn.shape),
                  _full_spec(g.shape), _full_spec(b.shape)],
        out_specs=[tile, pl.BlockSpec((1, FFN_CHUNKS, ctx_rows, FFN_CHUNK), lambda bi, i: (bi, 0, 0, 0))],
        out_shape=[jax.ShapeDtypeStruct((nb, rows, D_MODEL), F32),
                   jax.ShapeDtypeStruct((nb, FFN_CHUNKS, ctx_rows, FFN_CHUNK), F32)],
        scratch_shapes=[pltpu.VMEM((tm, D_MODEL), BF16), pltpu.VMEM((halo + tm, FFN_CHUNK), F32),
                        pltpu.VMEM((FFN_CHUNKS, halo, FFN_CHUNK), F32), pltpu.VMEM((tm, D_MODEL), F32)],
        compiler_params=_params(2),
        name="conv_ffn",
    )(x, ctx, wup, cw, cb, wdn, g, b)


def _lru_kernel(x_ref, cctx_ref, h0_ref, win_ref, cw_ref, cb_ref, wa_ref, ba_ref, wx_ref, bx_ref, lam_ref,
                wout_ref, g_ref, b_ref, y_ref, nctx_ref, nh_ref,
                ext_ref, a_ref, hb_ref, gate_ref, h_ref, *, tm, ts, n_tiles, end_off):
    i = pl.program_id(1)
    ctx_rows = (LRU_CONV - 1) * ts
    halo = _round_up(ctx_rows, SUBLANES)

    @pl.when(i == 0)
    def _():
        if halo > ctx_rows:
            ext_ref[0:halo - ctx_rows, :] = jnp.zeros((halo - ctx_rows, LRU_WIDTH), F32)
        ext_ref[halo - ctx_rows:halo, :] = cctx_ref[0]
        h_ref[...] = h0_ref[0]

    xb = x_ref[0].astype(BF16)
    gate_ref[...] = jax.nn.gelu(_dot(xb, win_ref[:, 0:LRU_WIDTH]))
    ext_ref[halo:halo + tm, :] = _dot(xb, win_ref[:, LRU_WIDTH:2 * LRU_WIDTH])

    lam = lam_ref[...]
    neg_c_softplus = -LRU_C * (jnp.maximum(-lam, 0.0) + jnp.log1p(jnp.exp(-jnp.abs(lam))))
    for n in range(LRU_BLOCKS):
        c = slice(n * LRU_BLOCK_DIM, (n + 1) * LRU_BLOCK_DIM)
        xc = cb_ref[:, c]
        for k in range(LRU_CONV):
            off = halo - (LRU_CONV - 1 - k) * ts
            xc = xc + cw_ref[k:k + 1, c] * ext_ref[off:off + tm, c]
        xcb = xc.astype(BF16)
        gate_r = jax.nn.sigmoid(_dot(xcb, wa_ref[n]) + ba_ref[:, c])
        gate_i = jax.nn.sigmoid(_dot(xcb, wx_ref[n]) + bx_ref[:, c])
        log_a = neg_c_softplus[:, c] * gate_r
        a_ref[:, c] = jnp.exp(log_a)
        t = jnp.tanh(-log_a)
        hb_ref[:, c] = jnp.sqrt(2.0 * t / (1.0 + t)) * gate_i * xc

    @pl.when(i == n_tiles - 1)
    def _():
        nctx_ref[0] = ext_ref[halo + end_off - ctx_rows:halo + end_off, :]

    if n_tiles > 1:
        ext_ref[0:halo, :] = ext_ref[tm:tm + halo, :]

    def step(t, h):
        r = pl.ds(pl.multiple_of(t * ts, ts), ts)
        hn = a_ref[r, :] * h + hb_ref[r, :]
        hb_ref[r, :] = hn
        return hn

    h_ref[...] = lax.fori_loop(0, tm // ts, step, h_ref[...], unroll=8 if tm // ts >= 8 else True)

    @pl.when(i == n_tiles - 1)
    def _():
        nh_ref[0] = hb_ref[end_off - ts:end_off, :]

    mixed = (gate_ref[...] * hb_ref[...]).astype(BF16)
    y_ref[0] = _layer_norm(DN_ALPHA * x_ref[0] + _dot(mixed, wout_ref[...]), g_ref[...], b_ref[...])


def _lru_call(x, cctx, h0, win, cw, cb, wa, ba, wx, bx, lam, wout, g, b, *, tm, ts, end):
    nb, rows, _ = x.shape
    n_tiles = rows // tm
    assert n_tiles == 1 or ts == 1
    assert rows - tm < end <= rows
    ctx_rows = (LRU_CONV - 1) * ts
    halo = _round_up(ctx_rows, SUBLANES)
    tile = pl.BlockSpec((1, tm, D_MODEL), lambda bi, i: (bi, i, 0))
    ctx_spec = pl.BlockSpec((1, ctx_rows, LRU_WIDTH), lambda bi, i: (bi, 0, 0))
    h_spec = pl.BlockSpec((1, ts, LRU_WIDTH), lambda bi, i: (bi, 0, 0))
    consts = [win, cw, cb, wa, ba, wx, bx, lam, wout, g, b]
    return pl.pallas_call(
        functools.partial(_lru_kernel, tm=tm, ts=ts, n_tiles=n_tiles, end_off=end - (rows - tm)),
        grid=(nb, n_tiles),
        in_specs=[tile, ctx_spec, h_spec] + [_full_spec(c.shape) for c in consts],
        out_specs=[tile, ctx_spec, h_spec],
        out_shape=[jax.ShapeDtypeStruct((nb, rows, D_MODEL), F32),
                   jax.ShapeDtypeStruct((nb, ctx_rows, LRU_WIDTH), F32),
                   jax.ShapeDtypeStruct((nb, ts, LRU_WIDTH), F32)],
        scratch_shapes=[pltpu.VMEM((halo + tm, LRU_WIDTH), F32), pltpu.VMEM((tm, LRU_WIDTH), F32),
                        pltpu.VMEM((tm, LRU_WIDTH), F32), pltpu.VMEM((tm, LRU_WIDTH), F32),
                        pltpu.VMEM((ts, LRU_WIDTH), F32)],
        compiler_params=_params(2),
        name="rg_lru",
    )(x, cctx, h0, *consts)


def _trunk(x, attend, states, w, *, tm, in_tile, ffn_tile, past_kv, ts, end, pos0):
    pool_ctx, lru_cctx, lru_h0, ffn_ctx0, ffn_ctx1 = states
    q, k, v, kb, vb, u = _sb_in_call(x, w["sb_w_in"], in_tile, *past_kv)
    oa = attend(q, k, v, kb, vb)
    x, new_pool = _sb_out_call(oa, u, pool_ctx, x, w["sb_w_out"], w["pool_w"], w["pool_scale"],
                               w["ln_g"][0], w["ln_b"][0], tm=tm, ts=ts, end=end, pos0=pos0)
    x, new_ffn0 = _ffn_call(x, ffn_ctx0, w["ffn_up"][0], w["ffn_cw"][0], w["ffn_cb"][0], w["ffn_dn"][0],
                            w["ln_g"][1], w["ln_b"][1], tm=ffn_tile, ts=ts, end=end)
    x, new_cctx, new_h = _lru_call(x, lru_cctx, lru_h0, w["lru_w_in"], w["lru_conv_w"], w["lru_conv_b"],
                                   w["lru_w_a"], w["lru_b_a"], w["lru_w_x"], w["lru_b_x"], w["lru_lambda"],
                                   w["lru_w_out"], w["ln_g"][2], w["ln_b"][2], tm=tm, ts=ts, end=end)
    x, new_ffn1 = _ffn_call(x, ffn_ctx1, w["ffn_up"][1], w["ffn_cw"][1], w["ffn_cb"][1], w["ffn_dn"][1],
                            w["ln_g"][3], w["ln_b"][3], tm=ffn_tile, ts=ts, end=end)
    return x, k, v, kb, vb, (new_pool, new_cctx, new_h, new_ffn0, new_ffn1)


def _ffn_ctx_rows(c):
    nb, _, r, _ = c.shape
    return c.transpose(0, 2, 1, 3).reshape(nb, r, FFN_DIM)


def _to_time_major(s):
    db, t, c = s.shape
    return s.transpose(1, 0, 2).reshape(1, t * db, c)


def _from_time_major(s, db):
    _, r, c = s.shape
    return s.reshape(r // db, db, c).transpose(1, 0, 2)


META_TILE = 128
PROMPT_TILE = 512
FFN_TILE = 1024
ATTN_TQ = 512
ATTN_TK = 256


def kernel(x_prompt, x_sample, cache_sb_k, cache_sb_v, page_table, state_pool, state_lru_conv, state_lru_h,
           state_ffn_conv, meta_tokens, sb_w_in, sb_logit_bias, sb_w_out, pool_w, pool_scale, lru_w_in,
           lru_conv_w, lru_conv_b, lru_w_a, lru_b_a, lru_w_x, lru_b_x, lru_lambda, lru_w_out, ffn_w_up,
           ffn_conv_w, ffn_conv_b, ffn_w_down, ln_g, ln_b):
    bp, seq, _ = x_prompt.shape
    db, t_new, _ = x_sample.shape
    row = lambda a: a.reshape(1, -1).astype(F32)
    w = dict(
        sb_w_in=sb_w_in[0].astype(BF16), sb_w_out=sb_w_out[0].astype(BF16), pool_w=pool_w[0].astype(BF16),
        pool_scale=row(pool_scale[0]),
        lru_w_in=lru_w_in[0].astype(BF16), lru_conv_w=lru_conv_w[0].astype(F32), lru_conv_b=row(lru_conv_b[0]),
        lru_w_a=lru_w_a[0].astype(BF16), lru_b_a=row(lru_b_a[0]), lru_w_x=lru_w_x[0].astype(BF16),
        lru_b_x=row(lru_b_x[0]), lru_lambda=row(lru_lambda[0]), lru_w_out=lru_w_out[0].astype(BF16),
        ffn_up=[ffn_w_up[l].reshape(D_MODEL, 2, FFN_CHUNKS, FFN_CHUNK).transpose(1, 2, 0, 3).astype(BF16)
                for l in range(DEPTH)],
        ffn_dn=[ffn_w_down[l].reshape(FFN_CHUNKS, FFN_CHUNK, D_MODEL).astype(BF16) for l in range(DEPTH)],
        ffn_cw=[ffn_conv_w[l].reshape(FFN_CONV, FFN_CHUNKS, FFN_CHUNK).transpose(1, 0, 2).astype(F32)
                for l in range(DEPTH)],
        ffn_cb=[ffn_conv_b[l].reshape(FFN_CHUNKS, 1, FFN_CHUNK).astype(F32) for l in range(DEPTH)],
        ln_g=[row(ln_g[l, s]) for l in range(DEPTH) for s in range(2)],
        ln_b=[row(ln_b[l, s]) for l in range(DEPTH) for s in range(2)],
    )
    bias2 = sb_logit_bias[0].astype(F32) * LOG2E

    x_meta = jnp.concatenate([meta_tokens.astype(F32), jnp.zeros((META_TILE - N_META, D_MODEL), F32)])[None]
    zero_states = (jnp.zeros((1, POOL_CTX, POOL_WIDTH), F32), jnp.zeros((1, LRU_CONV - 1, LRU_WIDTH), F32),
                   jnp.zeros((1, 1, LRU_WIDTH), F32), jnp.zeros((1, FFN_CONV - 1, FFN_DIM), F32),
                   jnp.zeros((1, FFN_CONV - 1, FFN_DIM), F32))
    no_past = jnp.zeros((META_TILE, SB_WIDTH), BF16)
    meta_attend = lambda q, k, v, kb, vb: _sb_attn_call(bias2, q, kb, vb, 0, META_TILE, META_TILE)
    _, k_meta, v_meta, kb_meta, vb_meta, st = _trunk(x_meta, meta_attend, zero_states, w, tm=META_TILE,
                                                      in_tile=META_TILE, ffn_tile=META_TILE, past_kv=(no_past, no_past), ts=1,
                                                      end=N_META, pos0=0)

    per_seq = lambda s: jnp.broadcast_to(s, (bp,) + s.shape[1:])
    prompt_states = (per_seq(st[0]), per_seq(st[1]), per_seq(st[2]),
                     per_seq(_ffn_ctx_rows(st[3])), per_seq(_ffn_ctx_rows(st[4])))
    meta_block = lambda a: jnp.concatenate([a[0, META_TILE:], jnp.zeros((ATTN_TK - META_TILE, SB_WIDTH), BF16)])
    prompt_attend = lambda q, k, v, kb, vb: _sb_attn_call(bias2, q, kb, vb, N_META, ATTN_TQ, ATTN_TK)
    y_prompt, k_real, v_real, _, _, sp = _trunk(x_prompt.astype(F32), prompt_attend, prompt_states, w,
                                                tm=PROMPT_TILE, in_tile=ATTN_TK, ffn_tile=FFN_TILE,
                                                past_kv=(meta_block(kb_meta), meta_block(vb_meta)), ts=1, end=seq,
                                                pos0=N_META)

    n_pages = page_table.shape[1]
    cache_kt = jnp.transpose(cache_sb_k, (0, 1, 3, 4, 2))
    cache_vt = jnp.transpose(cache_sb_v, (0, 1, 3, 4, 2))

    def sample_attend(q, k, v, kb, vb):
        split = lambda a: a.astype(F32).reshape(t_new, db, SB_HEADS, SB_HEAD_DIM)
        new_page = lambda a: jnp.pad(split(a).transpose(1, 2, 3, 0),
                                     ((0, 0), (0, 0), (0, 0), (0, PAGE_SIZE - t_new)))
        o = _sb_sample_call(page_table, split(q).transpose(1, 2, 0, 3), new_page(k), new_page(v), bias2,
                            cache_kt, cache_vt)
        return o.transpose(2, 0, 1, 3).reshape(1, t_new * db, SB_WIDTH).astype(BF16)

    sample_states = (_to_time_major(state_pool[0]), _to_time_major(state_lru_conv[0]), state_lru_h[0][None],
                     _to_time_major(state_ffn_conv[0]), _to_time_major(state_ffn_conv[1]))
    rows_s = db * t_new
    none = jnp.zeros((0, SB_WIDTH), BF16)
    y_s, k_s, v_s, _, _, ss = _trunk(_to_time_major(x_sample.astype(F32)), sample_attend, sample_states, w,
                                     tm=rows_s, in_tile=rows_s, ffn_tile=rows_s, past_kv=(none, none), ts=db, end=rows_s,
                                     pos0=n_pages * PAGE_SIZE)

    heads = lambda a: a.reshape(a.shape[:-1] + (SB_HEADS, SB_HEAD_DIM))
    with_meta = lambda m, r: heads(jnp.concatenate([per_seq(m[:, :N_META]), r], axis=1))[None]
    return (
        y_prompt,
        _from_time_major(y_s, db),
        with_meta(k_meta, k_real),
        with_meta(v_meta, v_real),
        heads(_from_time_major(k_s, db))[None],
        heads(_from_time_major(v_s, db))[None],
        sp[0][None],
        _from_time_major(ss[0], db)[None],
        sp[1][None],
        _from_time_major(ss[1], db)[None],
        sp[2].reshape(1, bp, LRU_WIDTH),
        ss[2].reshape(1, db, LRU_WIDTH),
        jnp.stack([_ffn_ctx_rows(sp[3]), _ffn_ctx_rows(sp[4])]),
        jnp.stack([_from_time_major(_ffn_ctx_rows(ss[3]), db), _from_time_major(_ffn_ctx_rows(ss[4]), db)]),
    )
```
